```python
import jax, jax.numpy as jnp
from jax import lax
import numpy as np

D_MODEL = 1024
BATCH = 4
SEQ = 4096
DEPTH = 2

N_EVEN = (DEPTH + 1) // 2
N_ODD = DEPTH // 2
D_FF = 2816
RMS_EPS = 1e-6
ROPE_THETA = 10000.0

GLA_HEADS = 4
GLA_DK = 64
GLA_DV = 128
GLA_GATE_RANK = 16
GLA_GATE_TAU = 16.0
GLA_CHUNK = 64

POOL_WINDOWS = (2, 4, 8, 16)
POOL_GROUPS = 4
POOL_GROUP_DIM = 128
POOL_DIM = POOL_GROUPS * POOL_GROUP_DIM

ATT_HEADS = 8
ATT_KV_HEADS = 2
ATT_HEAD_DIM = 128
IDX_HEADS = 4
IDX_DIM = 64
TOPK_MAX = 256
Q_BLOCK = 128

EVEN_SIZES = (GLA_HEADS * GLA_DK, GLA_HEADS * GLA_DK, GLA_HEADS * GLA_DV, GLA_HEADS * GLA_DV, GLA_GATE_RANK, POOL_DIM)
EVEN_IN = sum(EVEN_SIZES)
EVEN_SPLITS = tuple(int(s) for s in np.cumsum(EVEN_SIZES)[:-1])
EVEN_MIX = GLA_HEADS * GLA_DV + POOL_DIM
ODD_SIZES = (ATT_HEADS * ATT_HEAD_DIM, ATT_KV_HEADS * ATT_HEAD_DIM, ATT_KV_HEADS * ATT_HEAD_DIM, IDX_HEADS * IDX_DIM, IDX_DIM, IDX_HEADS)
ODD_IN = sum(ODD_SIZES)
ODD_SPLITS = tuple(int(s) for s in np.cumsum(ODD_SIZES)[:-1])
ODD_MIX = ATT_HEADS * ATT_HEAD_DIM

kernel_name = "hybrid_gla_pool_dsa_macaron"


def rms_norm(x, g):
    xf = x.astype(jnp.float32)
    y = xf * lax.rsqrt(jnp.mean(xf * xf, axis=-1, keepdims=True) + RMS_EPS)
    return (y * g.astype(jnp.float32)).astype(x.dtype)


def swiglu(x, wi, wo):
    gate, up = jnp.split(x @ wi, 2, axis=-1)
    return (jax.nn.silu(gate) * up) @ wo


def rope(x, positions):
    d = x.shape[-1]
    inv = ROPE_THETA ** (-jnp.arange(0, d, 2, dtype=jnp.float32) / d)
    ang = positions.astype(jnp.float32)[..., None] * inv
    cos = jnp.cos(ang)[:, :, None, :]
    sin = jnp.sin(ang)[:, :, None, :]
    xf = x.astype(jnp.float32)
    x1, x2 = jnp.split(xf, 2, axis=-1)
    out = jnp.concatenate([x1 * cos - x2 * sin, x2 * cos + x1 * sin], axis=-1)
    return out.astype(x.dtype)


def gla_chunked(q, k, v, log_a):
    B, S, H, DK = q.shape
    DV = v.shape[-1]
    C = GLA_CHUNK
    n = S // C

    def to_chunks(t):
        return t.astype(jnp.float32).reshape(B, n, C, H, t.shape[-1]).transpose(1, 0, 3, 2, 4)

    qc = to_chunks(q.astype(jnp.float32) * (DK ** -0.5))
    kc, vc, ac = to_chunks(k), to_chunks(v), to_chunks(log_a)
    causal = jnp.tril(jnp.ones((C, C), dtype=bool))[:, :, None]

    def step(state, inp):
        qb, kb, vb, ab = inp
        b = jnp.cumsum(ab, axis=2)
        inter = jnp.einsum('bhcd,bhde->bhce', qb * jnp.exp(b), state)
        rel = jnp.where(causal, b[:, :, :, None, :] - b[:, :, None, :, :], -jnp.inf)
        scores = jnp.einsum('bhid,bhjd,bhijd->bhij', qb, kb, jnp.exp(rel))
        intra = jnp.einsum('bhij,bhje->bhie', scores, vb)
        b_last = b[:, :, -1, :]
        k_dec = kb * jnp.exp(b_last[:, :, None, :] - b)
        state = jnp.exp(b_last)[..., None] * state + jnp.einsum('bhcd,bhce->bhde', k_dec, vb)
        return state, inter + intra

    s0 = jnp.zeros((B, H, DK, DV), jnp.float32)
    _, out = lax.scan(step, s0, (qc, kc, vc, ac))
    return out.transpose(1, 0, 3, 2, 4).reshape(B, S, H, DV)


def multiscale_pool(u):
    B, S, _ = u.shape
    ug = u.astype(jnp.float32).reshape(B, S, POOL_GROUPS, POOL_GROUP_DIM)
    csum = jnp.cumsum(ug, axis=1)
    t = jnp.arange(1, S + 1, dtype=jnp.float32)
    outs = []
    for g, w in enumerate(POOL_WINDOWS):
        c = csum[:, :, g]
        c_prev = jnp.pad(c, ((0, 0), (w, 0), (0, 0)))[:, :S]
        count = jnp.minimum(t, float(w))[None, :, None]
        outs.append((c - c_prev) / count - ug[:, :, g])
    return jnp.stack(outs, axis=2)


def gla_pool_mixer(hn, w_in, gate_w, gate_b, out_norm, pool_w, pool_scale, w_out):
    B, S, _ = hn.shape
    proj = hn @ w_in
    q, k, v, g, a_lr, u = jnp.split(proj, EVEN_SPLITS, axis=-1)
    log_a = jax.nn.log_sigmoid((a_lr @ gate_w + gate_b).astype(jnp.float32)) / GLA_GATE_TAU
    o = gla_chunked(q.reshape(B, S, GLA_HEADS, GLA_DK), k.reshape(B, S, GLA_HEADS, GLA_DK),
                    v.reshape(B, S, GLA_HEADS, GLA_DV), log_a.reshape(B, S, GLA_HEADS, GLA_DK))
    o = rms_norm(o, out_norm.reshape(GLA_HEADS, GLA_DV)).reshape(B, S, GLA_HEADS * GLA_DV)
    o = (o * jax.nn.silu(g.astype(jnp.float32))).astype(hn.dtype)
    p = multiscale_pool(u)
    p = jnp.einsum('bsgc,gcd->bsgd', p, pool_w.astype(jnp.float32)).reshape(B, S, POOL_DIM)
    p = (p * pool_scale.astype(jnp.float32)).astype(hn.dtype)
    return jnp.concatenate([o, p], axis=-1) @ w_out


def dsa_attention(q, k, v, qi, ki, wi):
    B, S, H, Dh = q.shape
    G = k.shape[2]
    R = H // G
    top_k = min(TOPK_MAX, S // 4)
    n_blocks = S // Q_BLOCK
    idx_scale = (IDX_DIM ** -0.5) * (IDX_HEADS ** -0.5)
    key_pos = jnp.arange(S)
    ki_f = ki.astype(jnp.float32)

    def block(i):
        start = i * Q_BLOCK
        qb = lax.dynamic_slice_in_dim(q, start, Q_BLOCK, axis=1).reshape(B, Q_BLOCK, G, R, Dh)
        qib = lax.dynamic_slice_in_dim(qi, start, Q_BLOCK, axis=1).astype(jnp.float32)
        wib = lax.dynamic_slice_in_dim(wi, start, Q_BLOCK, axis=1).astype(jnp.float32)
        q_pos = start + jnp.arange(Q_BLOCK)
        logits = jax.nn.relu(jnp.einsum('bthd,bsd->bths', qib, ki_f))
        iscore = jnp.einsum('bth,bths->bts', wib, logits) * idx_scale
        admissible = key_pos[None, :] <= q_pos[:, None]
        iscore = jnp.where(admissible[None], iscore, -jnp.inf)
        _, sel = lax.top_k(iscore, top_k)
        valid = sel <= q_pos[None, :, None]
        k_sel = jax.vmap(lambda a, ix: a[ix])(k, sel)
        v_sel = jax.vmap(lambda a, ix: a[ix])(v, sel)
        s = jnp.einsum('btgrd,btkgd->btgrk', qb.astype(jnp.float32), k_sel.astype(jnp.float32)) * (Dh ** -0.5)
        s = jnp.where(valid[:, :, None, None, :], s, -jnp.inf)
        p = jax.nn.softmax(s, axis=-1)
        o = jnp.einsum('btgrk,btkgd->btgrd', p, v_sel.astype(jnp.float32))
        return o.reshape(B, Q_BLOCK, H * Dh).astype(q.dtype)

    out = lax.map(block, jnp.arange(n_blocks))
    return out.transpose(1, 0, 2, 3).reshape(B, S, H * Dh)


def dsa_mixer(hn, positions, w_in, w_out):
    B, S, _ = hn.shape
    proj = hn @ w_in
    q, k, v, qi, ki, wi = jnp.split(proj, ODD_SPLITS, axis=-1)
    q = rope(q.reshape(B, S, ATT_HEADS, ATT_HEAD_DIM), positions)
    k = rope(k.reshape(B, S, ATT_KV_HEADS, ATT_HEAD_DIM), positions)
    v = v.reshape(B, S, ATT_KV_HEADS, ATT_HEAD_DIM)
    qi = rope(qi.reshape(B, S, IDX_HEADS, IDX_DIM), positions)
    ki = rope(ki.reshape(B, S, 1, IDX_DIM), positions)[:, :, 0]
    o = dsa_attention(q, k, v, qi, ki, wi)
    return o @ w_out


def setup_inputs(seed: int = 0) -> dict:
    key = jax.random.key(seed)
    ks = jax.random.split(key, 20)
    f32 = jnp.float32

    def nrm(k, shape, scale):
        return jax.random.normal(k, shape, f32) * scale

    def gain(k, shape):
        return 1.0 + 0.02 * jax.random.normal(k, shape, f32)

    x = jax.random.normal(ks[0], (BATCH, SEQ, D_MODEL), f32)
    positions = jnp.broadcast_to(jnp.arange(SEQ, dtype=jnp.int32)[None, :], (BATCH, SEQ)).astype(jnp.int32)
    return {
        "x": x,
        "positions": positions,
        "ffn1_norm": gain(ks[1], (DEPTH, D_MODEL)),
        "ffn1_wi": nrm(ks[2], (DEPTH, D_MODEL, 2 * D_FF), D_MODEL ** -0.5),
        "ffn1_wo": nrm(ks[3], (DEPTH, D_FF, D_MODEL), D_FF ** -0.5),
        "mix_norm": gain(ks[4], (DEPTH, D_MODEL)),
        "ffn2_norm": gain(ks[5], (DEPTH, D_MODEL)),
        "ffn2_wi": nrm(ks[6], (DEPTH, D_MODEL, 2 * D_FF), D_MODEL ** -0.5),
        "ffn2_wo": nrm(ks[7], (DEPTH, D_FF, D_MODEL), D_FF ** -0.5),
        "even_w_in": nrm(ks[8], (N_EVEN, D_MODEL, EVEN_IN), D_MODEL ** -0.5),
        "gla_gate_w": nrm(ks[9], (N_EVEN, GLA_GATE_RANK, GLA_HEADS * GLA_DK), GLA_GATE_RANK ** -0.5),
        "gla_gate_b": nrm(ks[10], (N_EVEN, GLA_HEADS * GLA_DK), 0.1),
        "gla_out_norm": gain(ks[11], (N_EVEN, GLA_HEADS * GLA_DV)),
        "pool_w": nrm(ks[12], (N_EVEN, POOL_GROUPS, POOL_GROUP_DIM, POOL_GROUP_DIM), POOL_GROUP_DIM ** -0.5),
        "pool_scale": gain(ks[13], (N_EVEN, POOL_DIM)),
        "even_w_out": nrm(ks[14], (N_EVEN, EVEN_MIX, D_MODEL), EVEN_MIX ** -0.5),
        "odd_w_in": nrm(ks[15], (N_ODD, D_MODEL, ODD_IN), D_MODEL ** -0.5),
        "odd_w_out": nrm(ks[16], (N_ODD, ODD_MIX, D_MODEL), ODD_MIX ** -0.5),
        "final_norm": gain(ks[17], (D_MODEL,)),
    }


def reference(x, positions, ffn1_norm, ffn1_wi, ffn1_wo, mix_norm, ffn2_norm, ffn2_wi, ffn2_wo,
              even_w_in, gla_gate_w, gla_gate_b, gla_out_norm, pool_w, pool_scale, even_w_out,
              odd_w_in, odd_w_out, final_norm):
    h = x
    for li in range(DEPTH):
        h = h + 0.5 * swiglu(rms_norm(h, ffn1_norm[li]), ffn1_wi[li], ffn1_wo[li])
        hn = rms_norm(h, mix_norm[li])
        if li % 2 == 0:
            j = li // 2
            mix = gla_pool_mixer(hn, even_w_in[j], gla_gate_w[j], gla_gate_b[j], gla_out_norm[j],
                                 pool_w[j], pool_scale[j], even_w_out[j])
        else:
            j = li // 2
            mix = dsa_mixer(hn, positions, odd_w_in[j], odd_w_out[j])
        h = h + mix
        h = h + 0.5 * swiglu(rms_norm(h, ffn2_norm[li]), ffn2_wi[li], ffn2_wo[li])
    return rms_norm(h, final_norm)
```

```python
import functools
import math

import jax
import jax.numpy as jnp
import numpy as np
from jax import lax
from jax.experimental import pallas as pl
from jax.experimental.pallas import tpu as pltpu

F32 = jnp.float32
BF16 = jnp.bfloat16

RMS_EPS = 1e-6
ROPE_THETA = 10000.0

GLA_HEADS = 4
GLA_DK = 64
GLA_DV = 128
GLA_GATE_RANK = 16
GLA_GATE_TAU = 16.0

POOL_WINDOWS = (2, 4, 8, 16)
POOL_GROUP_DIM = 128
POOL_HALO = 16

ATT_HEADS = 8
ATT_KV_HEADS = 2
ATT_HEAD_DIM = 128
IDX_HEADS = 4
IDX_DIM = 64
TOPK_MAX = 256
Q_BLOCK = 128

LANES = 128
MASKED_SCORE = -1e30
VMEM_LIMIT = 56 * 1024 * 1024


def _rms(x, g):
    ms = jnp.mean(x * x, axis=-1, keepdims=True)
    return x * lax.rsqrt(ms + RMS_EPS) * g


def _dot(a, b):
    return jnp.dot(a, b, preferred_element_type=F32)


def _dot_nt(a, b):
    return lax.dot_general(a, b, (((1,), (1,)), ((), ())), preferred_element_type=F32)


def _dot_tn(a, b):
    return lax.dot_general(a, b, (((0,), (0,)), ((), ())), preferred_element_type=F32)


def _const_spec(shape):
    n = len(shape)
    return pl.BlockSpec(shape, lambda *_: (0,) * n)


def _ffn_kernel(h_ref, g_ref, wg_ref, wu_ref, wo_ref, fg_ref, o_ref, xn_ref, acc_ref, *, n_chunks, final):
    x = h_ref[...]
    xn_ref[...] = _rms(x, g_ref[...]).astype(BF16)
    acc_ref[...] = jnp.zeros_like(acc_ref)

    def body(c, carry):
        xn = xn_ref[...]
        gate = _dot(xn, wg_ref[c])
        up = _dot(xn, wu_ref[c])
        act = gate * (1.0 / (1.0 + jnp.exp(-gate))) * up
        acc_ref[...] += _dot(act.astype(BF16), wo_ref[c])
        return carry

    lax.fori_loop(0, n_chunks, body, 0)
    out = x + 0.5 * acc_ref[...]
    if final:
        out = _rms(out, fg_ref[...])
    o_ref[...] = out


def _ffn(h, norm_g, wi, wo, final_g, *, final, tm=512, tf=256):
    n, d = h.shape
    f = wo.shape[0]
    n_chunks = f // tf
    wg = wi[:, :f].astype(BF16).reshape(d, n_chunks, tf).transpose(1, 0, 2)
    wu = wi[:, f:].astype(BF16).reshape(d, n_chunks, tf).transpose(1, 0, 2)
    wo3 = wo.astype(BF16).reshape(n_chunks, tf, d)
    return pl.pallas_call(
        functools.partial(_ffn_kernel, n_chunks=n_chunks, final=final),
        grid=(n // tm,),
        in_specs=[
            pl.BlockSpec((tm, d), lambda i: (i, 0)),
            _const_spec((1, d)),
            _const_spec((n_chunks, d, tf)),
            _const_spec((n_chunks, d, tf)),
            _const_spec((n_chunks, tf, d)),
            _const_spec((1, d)),
        ],
        out_specs=pl.BlockSpec((tm, d), lambda i: (i, 0)),
        out_shape=jax.ShapeDtypeStruct((n, d), F32),
        scratch_shapes=[pltpu.VMEM((tm, d), BF16), pltpu.VMEM((tm, d), F32)],
        compiler_params=pltpu.CompilerParams(
            dimension_semantics=("arbitrary",), vmem_limit_bytes=VMEM_LIMIT),
        name="ffn_final" if final else "ffn",
    )(h, norm_g.reshape(1, d), wg, wu, wo3, final_g.reshape(1, d))


def _log_sigmoid(z):
    return jnp.minimum(z, 0.0) - jnp.log1p(jnp.exp(-jnp.abs(z)))


def _even_kernel(h_ref, g_ref, wm_ref, wa_ref, gw_ref, gb_ref, on_ref, pw_ref, ps_ref, wo_ref,
                 o_ref, state_ref, ext_ref, mix_ref, *, tt):
    t_idx = pl.program_id(1)
    hk = GLA_HEADS * GLA_DK
    hv = GLA_HEADS * GLA_DV

    @pl.when(t_idx == 0)
    def _():
        state_ref[...] = jnp.zeros_like(state_ref)
        ext_ref[0:POOL_HALO, :] = jnp.zeros((POOL_HALO, ext_ref.shape[1]), F32)

    x = h_ref[0]
    hn = _rms(x, g_ref[...]).astype(BF16)
    proj = _dot(hn, wm_ref[...])
    a_lr = _dot(hn, wa_ref[...])
    z = jnp.dot(a_lr, gw_ref[...], preferred_element_type=F32, precision=lax.Precision.HIGHEST) + gb_ref[...]
    la = _log_sigmoid(z) * (1.0 / GLA_GATE_TAU)

    row = lax.broadcasted_iota(jnp.int32, (tt, tt), 0)
    col = lax.broadcasted_iota(jnp.int32, (tt, tt), 1)
    tril = jnp.where(col <= row, 1.0, 0.0).astype(F32)
    bc = jnp.dot(tril, la, preferred_element_type=F32, precision=lax.Precision.HIGHEST)

    levels = []
    s = tt
    while s >= 2:
        levels.append(s)
        s //= 2
    rowk = lax.broadcasted_iota(jnp.int32, (tt, hk), 0)
    q_scaled = []
    k_scaled = []
    lvl_masks = []
    for s in levels:
        half = s // 2
        sh = s.bit_length() - 1
        gsel = jnp.where(col == ((row >> sh) << sh) + (half - 1), 1.0, 0.0).astype(F32)
        ref_b = jnp.dot(gsel, bc, preferred_element_type=F32, precision=lax.Precision.HIGHEST)
        upper = (rowk & (s - 1)) >= half
        q_scaled.append(jnp.exp(jnp.where(upper, bc - ref_b, 0.0)))
        k_scaled.append(jnp.exp(jnp.where(upper, 0.0, ref_b - bc)))
        lvl_masks.append((((row ^ col) >> sh) == 0) & ((row & (s - 1)) >= half) & ((col & (s - 1)) < half))

    e_b = jnp.exp(bc)
    b_last = bc[tt - 1:tt, :]
    e_last = jnp.exp(b_last)
    e_dec = jnp.exp(b_last - bc)

    for hd in range(GLA_HEADS):
        ks = slice(hd * GLA_DK, (hd + 1) * GLA_DK)
        qh = proj[:, ks] * (GLA_DK ** -0.5)
        kh = proj[:, hk + hd * GLA_DK: hk + (hd + 1) * GLA_DK]
        vh = proj[:, 2 * hk + hd * GLA_DV: 2 * hk + (hd + 1) * GLA_DV]
        vh_b = vh.astype(BF16)
        scores = jnp.zeros((tt, tt), F32)
        for li in range(len(levels)):
            qs = (qh * q_scaled[li][:, ks]).astype(BF16)
            kss = (kh * k_scaled[li][:, ks]).astype(BF16)
            scores = scores + jnp.where(lvl_masks[li], _dot_nt(qs, kss), 0.0)
        diag = jnp.sum(qh * kh, axis=-1, keepdims=True)
        st = state_ref[hd]
        inter = _dot_nt((qh * e_b[:, ks]).astype(BF16), st.astype(BF16))
        o_h = inter + _dot(scores.astype(BF16), vh_b) + diag * vh
        state_ref[hd] = st * e_last[:, ks] + _dot_tn(vh_b, (kh * e_dec[:, ks]).astype(BF16))
        gh = proj[:, 2 * hk + hv + hd * GLA_DV: 2 * hk + hv + (hd + 1) * GLA_DV]
        o_n = _rms(o_h, on_ref[:, hd * GLA_DV:(hd + 1) * GLA_DV])
        mix_ref[:, hd * GLA_DV:(hd + 1) * GLA_DV] = (o_n * gh * (1.0 / (1.0 + jnp.exp(-gh)))).astype(BF16)

    u = proj[:, 2 * hk + 2 * hv:]
    ext_ref[POOL_HALO:POOL_HALO + tt, :] = u
    tpos = (t_idx * tt + lax.broadcasted_iota(jnp.int32, (tt, 1), 0) + 1).astype(F32)
    for gi, w in enumerate(POOL_WINDOWS):
        ls = slice(gi * POOL_GROUP_DIM, (gi + 1) * POOL_GROUP_DIM)
        ug = ext_ref[POOL_HALO:POOL_HALO + tt, ls]
        win = ug
        for dlt in range(1, w):
            win = win + ext_ref[POOL_HALO - dlt:POOL_HALO - dlt + tt, ls]
        pg = win / jnp.minimum(tpos, float(w)) - ug
        pg = _dot(pg.astype(BF16), pw_ref[gi]) * ps_ref[:, ls]
        mix_ref[:, hv + gi * POOL_GROUP_DIM: hv + (gi + 1) * POOL_GROUP_DIM] = pg.astype(BF16)
    ext_ref[0:POOL_HALO, :] = ext_ref[tt:tt + POOL_HALO, :]

    o_ref[0] = x + _dot(mix_ref[...], wo_ref[...])


def _even_mixer(h, norm_g, w_in, gate_w, gate_b, out_norm, pool_w, pool_scale, w_out, *, tt=256):
    b, s, d = h.shape
    hk = GLA_HEADS * GLA_DK
    hv = GLA_HEADS * GLA_DV
    pool_dim = len(POOL_WINDOWS) * POOL_GROUP_DIM
    o_a = 2 * hk + 2 * hv
    w_main = jnp.concatenate([w_in[:, :o_a], w_in[:, o_a + GLA_GATE_RANK:]], axis=1).astype(BF16)
    w_a = jnp.pad(w_in[:, o_a:o_a + GLA_GATE_RANK], ((0, 0), (0, LANES - GLA_GATE_RANK))).astype(BF16)
    gw = jnp.pad(gate_w, ((0, LANES - GLA_GATE_RANK), (0, 0)))
    n_main = w_main.shape[1]
    mix_dim = hv + pool_dim
    return pl.pallas_call(
        functools.partial(_even_kernel, tt=tt),
        grid=(b, s // tt),
        in_specs=[
            pl.BlockSpec((1, tt, d), lambda bi, ti: (bi, ti, 0)),
            _const_spec((1, d)),
            _const_spec((d, n_main)),
            _const_spec((d, LANES)),
            _const_spec((LANES, hk)),
            _const_spec((1, hk)),
            _const_spec((1, hv)),
            _const_spec((len(POOL_WINDOWS), POOL_GROUP_DIM, POOL_GROUP_DIM)),
            _const_spec((1, pool_dim)),
            _const_spec((mix_dim, d)),
        ],
        out_specs=pl.BlockSpec((1, tt, d), lambda bi, ti: (bi, ti, 0)),
        out_shape=jax.ShapeDtypeStruct((b, s, d), F32),
        scratch_shapes=[
            pltpu.VMEM((GLA_HEADS, GLA_DV, GLA_DK), F32),
            pltpu.VMEM((POOL_HALO + tt, pool_dim), F32),
            pltpu.VMEM((tt, mix_dim), BF16),
        ],
        compiler_params=pltpu.CompilerParams(
            dimension_semantics=("arbitrary", "arbitrary"), vmem_limit_bytes=VMEM_LIMIT),
        name="even_mixer",
    )(h, norm_g.reshape(1, d), w_main, w_a, gw, gate_b.reshape(1, hk), out_norm.reshape(1, hv),
      pool_w.astype(BF16), pool_scale.reshape(1, pool_dim), w_out.astype(BF16))


def _odd_proj_kernel(h_ref, g_ref, w_ref, pos_ref, inv_ref, q_ref, k_ref, v_ref, qi_ref, ki_ref, wi_ref, *, tm):
    nq = ATT_HEADS * ATT_HEAD_DIM
    nkv = ATT_KV_HEADS * ATT_HEAD_DIM
    ni = IDX_HEADS * IDX_DIM
    hn = _rms(h_ref[0], g_ref[...]).astype(BF16)
    proj = _dot(hn, w_ref[...])
    pos = pos_ref[0]
    lane = lax.broadcasted_iota(jnp.int32, (tm, LANES), 1)

    ang_a = pos * inv_ref[0:1, :]
    cos_a = jnp.cos(ang_a)
    sin_a = jnp.sin(ang_a)
    sin_a = jnp.where(lane < ATT_HEAD_DIM // 2, -sin_a, sin_a)

    def rope_att(xh):
        return xh * cos_a + pltpu.roll(xh, ATT_HEAD_DIM // 2, 1) * sin_a

    ang_i = pos * inv_ref[1:2, :]
    cos_i = jnp.cos(ang_i)
    sin_i = jnp.sin(ang_i)
    low_half = (lane % IDX_DIM) < IDX_DIM // 2

    def rope_idx(xb):
        rot = jnp.where(low_half, -pltpu.roll(xb, LANES - IDX_DIM // 2, 1), pltpu.roll(xb, IDX_DIM // 2, 1))
        return xb * cos_i + rot * sin_i

    for hd in range(ATT_HEADS):
        xh = proj[:, hd * ATT_HEAD_DIM:(hd + 1) * ATT_HEAD_DIM]
        q_ref[0, hd] = (rope_att(xh) * (ATT_HEAD_DIM ** -0.5)).astype(BF16)
    for gi in range(ATT_KV_HEADS):
        xh = proj[:, nq + gi * ATT_HEAD_DIM: nq + (gi + 1) * ATT_HEAD_DIM]
        k_ref[0, :, gi * ATT_HEAD_DIM:(gi + 1) * ATT_HEAD_DIM] = rope_att(xh).astype(BF16)
    v_ref[0] = proj[:, nq + nkv: nq + 2 * nkv].astype(BF16)
    o_i = nq + 2 * nkv
    for pi in range(ni // LANES):
        xr = rope_idx(proj[:, o_i + pi * LANES: o_i + (pi + 1) * LANES]).astype(BF16)
        for sub in range(LANES // IDX_DIM):
            qi_ref[0, pi * (LANES // IDX_DIM) + sub] = xr[:, sub * IDX_DIM:(sub + 1) * IDX_DIM]
    tail = proj[:, o_i + ni: o_i + ni + LANES]
    ki_ref[0] = rope_idx(tail)[:, :IDX_DIM].astype(BF16)
    wi_ref[0] = pltpu.roll(tail, LANES - IDX_DIM, 1)


def _odd_proj(h, norm_g, w_in, positions, *, tm=512):
    b, s, d = h.shape
    n_in = w_in.shape[1]
    n_pad = -(-n_in // LANES) * LANES
    w = jnp.pad(w_in, ((0, 0), (0, n_pad - n_in))).astype(BF16)
    inv_a = ROPE_THETA ** (-jnp.arange(0, ATT_HEAD_DIM, 2, dtype=F32) / ATT_HEAD_DIM)
    inv_i = ROPE_THETA ** (-jnp.arange(0, IDX_DIM, 2, dtype=F32) / IDX_DIM)
    inv = jnp.stack([jnp.tile(inv_a, LANES // inv_a.shape[0]), jnp.tile(inv_i, LANES // inv_i.shape[0])])
    inv = jnp.pad(inv, ((0, 6), (0, 0)))
    pos = positions.astype(F32).reshape(b, s, 1)
    return pl.pallas_call(
        functools.partial(_odd_proj_kernel, tm=tm),
        grid=(b, s // tm),
        in_specs=[
            pl.BlockSpec((1, tm, d), lambda bi, ti: (bi, ti, 0)),
            _const_spec((1, d)),
            _const_spec((d, n_pad)),
            pl.BlockSpec((1, tm, 1), lambda bi, ti: (bi, ti, 0)),
            _const_spec((8, LANES)),
        ],
        out_specs=[
            pl.BlockSpec((1, ATT_HEADS, tm, ATT_HEAD_DIM), lambda bi, ti: (bi, 0, ti, 0)),
            pl.BlockSpec((1, tm, ATT_KV_HEADS * ATT_HEAD_DIM), lambda bi, ti: (bi, ti, 0)),
            pl.BlockSpec((1, tm, ATT_KV_HEADS * ATT_HEAD_DIM), lambda bi, ti: (bi, ti, 0)),
            pl.BlockSpec((1, IDX_HEADS, tm, IDX_DIM), lambda bi, ti: (bi, 0, ti, 0)),
            pl.BlockSpec((1, tm, IDX_DIM), lambda bi, ti: (bi, ti, 0)),
            pl.BlockSpec((1, tm, LANES), lambda bi, ti: (bi, ti, 0)),
        ],
        out_shape=[
            jax.ShapeDtypeStruct((b, ATT_HEADS, s, ATT_HEAD_DIM), BF16),
            jax.ShapeDtypeStruct((b, s, ATT_KV_HEADS * ATT_HEAD_DIM), BF16),
            jax.ShapeDtypeStruct((b, s, ATT_KV_HEADS * ATT_HEAD_DIM), BF16),
            jax.ShapeDtypeStruct((b, IDX_HEADS, s, IDX_DIM), BF16),
            jax.ShapeDtypeStruct((b, s, IDX_DIM), BF16),
            jax.ShapeDtypeStruct((b, s, LANES), F32),
        ],
        compiler_params=pltpu.CompilerParams(
            dimension_semantics=("arbitrary", "arbitrary"), vmem_limit_bytes=VMEM_LIMIT),
        name="odd_proj",
    )(h, norm_g.reshape(1, d), w, pos, inv)


def _ordered_to_float(o):
    bits = jnp.where(o >= 0, o, o ^ jnp.int32(0x7FFFFFFF))
    return lax.bitcast_convert_type(bits, F32)


def _dsa_kernel(q_ref, qi_ref, wi_ref, k_ref, v_ref, ki_ref, h_ref, wo_ref, o_ref,
                sc_ref, m_ref, l_ref, acc_ref, og_ref, *, tq, kc, top_k, idx_bits):
    qb = pl.program_id(1)
    n_kc = (qb * tq + tq + kc - 1) // kc
    rep = ATT_HEADS // ATT_KV_HEADS
    n_sub = kc // LANES
    q_pos = qb * tq + lax.broadcasted_iota(jnp.int32, (tq, 1), 0)
    key_iota = lax.broadcasted_iota(jnp.int32, (tq, kc), 1)
    idx_scale = (IDX_DIM ** -0.5) * (IDX_HEADS ** -0.5)

    qi = qi_ref[0].reshape(IDX_HEADS * tq, IDX_DIM)
    wi = wi_ref[0]

    def score_body(c, carry):
        off = pl.multiple_of(c * kc, kc)
        lg = _dot_nt(qi, ki_ref[0, pl.ds(off, kc), :])
        lg = jnp.maximum(lg, 0.0).reshape(IDX_HEADS, tq, kc)
        isc = lg[0] * wi[:, 0:1]
        for hd in range(1, IDX_HEADS):
            isc = isc + lg[hd] * wi[:, hd:hd + 1]
        isc = isc * idx_scale
        sc_ref[c] = jnp.where(key_iota + off <= q_pos, isc, -jnp.inf)
        return carry

    lax.fori_loop(0, n_kc, score_body, 0)

    def lane_partial(vals):
        acc = vals[:, 0:LANES]
        for j in range(1, n_sub):
            acc = acc + vals[:, j * LANES:(j + 1) * LANES]
        return acc

    def count(hit):
        def body(c, acc):
            return acc + lane_partial(hit(sc_ref[c], c * kc))
        part = lax.fori_loop(0, n_kc, body, jnp.zeros((tq, LANES), F32))
        return jnp.sum(part, axis=-1, keepdims=True)

    kf = float(top_k)
    lo0 = jnp.full((tq, 1), np.int32(-2139095041))
    hi0 = jnp.full((tq, 1), np.int32(2139095041))

    def bis_body(_, carry):
        lo, hi = carry
        mid = (lo >> 1) + (hi >> 1) + (lo & hi & 1)
        thr = _ordered_to_float(mid)
        ok = count(lambda xs, off: jnp.where(xs >= thr, 1.0, 0.0)) >= kf
        return jnp.where(ok, mid, lo), jnp.where(ok, hi, mid)

    lo, _ = lax.fori_loop(0, 32, bis_body, (lo0, hi0))
    thr = _ordered_to_float(lo)
    need = kf - count(lambda xs, off: jnp.where(xs > thr, 1.0, 0.0))

    def cut_body(it, cut):
        cand = cut | (jnp.int32(1) << (jnp.int32(idx_bits - 1) - it))
        below = count(lambda xs, off: jnp.where(xs == thr, jnp.where(key_iota + off < cand, 1.0, 0.0), 0.0))
        return jnp.where(below < need, cand, cut)

    cut = lax.fori_loop(0, idx_bits, cut_body, jnp.zeros((tq, 1), jnp.int32))

    def mask_body(c, carry):
        off = c * kc
        xs = sc_ref[c]
        kidx = key_iota + off
        sel = jnp.where(xs > thr, 1.0, jnp.where(xs == thr, jnp.where(kidx <= cut, 1.0, 0.0), 0.0))
        sc_ref[c] = jnp.where(kidx <= q_pos, sel, 0.0)
        return carry

    lax.fori_loop(0, n_kc, mask_body, 0)

    for gi in range(ATT_KV_HEADS):
        qg = q_ref[0, gi * rep:(gi + 1) * rep].reshape(rep * tq, ATT_HEAD_DIM)
        m_ref[...] = jnp.full(m_ref.shape, MASKED_SCORE, F32)
        l_ref[...] = jnp.zeros_like(l_ref)
        acc_ref[...] = jnp.zeros_like(acc_ref)
        ls = slice(gi * ATT_HEAD_DIM, (gi + 1) * ATT_HEAD_DIM)

        def att_body(c, carry):
            off = pl.multiple_of(c * kc, kc)
            kb = k_ref[0, pl.ds(off, kc), ls]
            vb = v_ref[0, pl.ds(off, kc), ls]
            s = _dot_nt(qg, kb).reshape(rep, tq, kc)
            s = jnp.where(sc_ref[c][None] > 0.0, s, MASKED_SCORE).reshape(rep * tq, kc)
            m_old = m_ref[...]
            m_new = jnp.maximum(m_old, jnp.max(s, axis=-1, keepdims=True))
            alpha = jnp.exp(m_old - m_new)
            p = jnp.exp(s - m_new)
            l_ref[...] = alpha * l_ref[...] + jnp.sum(p, axis=-1, keepdims=True)
            acc_ref[...] = alpha * acc_ref[...] + _dot(p.astype(BF16), vb)
            m_ref[...] = m_new
            return carry

        lax.fori_loop(0, n_kc, att_body, 0)
        og = (acc_ref[...] / l_ref[...]).astype(BF16).reshape(rep, tq, ATT_HEAD_DIM)
        for r in range(rep):
            hd = gi * rep + r
            og_ref[:, hd * ATT_HEAD_DIM:(hd + 1) * ATT_HEAD_DIM] = og[r]

    o_ref[0] = h_ref[0] + _dot(og_ref[...], wo_ref[...])


def _dsa(h, q, k, v, qi, ki, wi, w_out, *, kc=512):
    b, s, d = h.shape
    tq = Q_BLOCK
    top_k = min(TOPK_MAX, s // 4)
    kc = min(kc, s)
    assert kc >= top_k and s % kc == 0 and kc % tq == 0
    rep = ATT_HEADS // ATT_KV_HEADS
    nkv = ATT_KV_HEADS * ATT_HEAD_DIM
    return pl.pallas_call(
        functools.partial(_dsa_kernel, tq=tq, kc=kc, top_k=top_k, idx_bits=(s - 1).bit_length()),
        grid=(b, s // tq),
        in_specs=[
            pl.BlockSpec((1, ATT_HEADS, tq, ATT_HEAD_DIM), lambda bi, qi_: (bi, 0, qi_, 0)),
            pl.BlockSpec((1, IDX_HEADS, tq, IDX_DIM), lambda bi, qi_: (bi, 0, qi_, 0)),
            pl.BlockSpec((1, tq, LANES), lambda bi, qi_: (bi, qi_, 0)),
            pl.BlockSpec((1, s, nkv), lambda bi, qi_: (bi, 0, 0)),
            pl.BlockSpec((1, s, nkv), lambda bi, qi_: (bi, 0, 0)),
            pl.BlockSpec((1, s, IDX_DIM), lambda bi, qi_: (bi, 0, 0)),
            pl.BlockSpec((1, tq, d), lambda bi, qi_: (bi, qi_, 0)),
            _const_spec((ATT_HEADS * ATT_HEAD_DIM, d)),
        ],
        out_specs=pl.BlockSpec((1, tq, d), lambda bi, qi_: (bi, qi_, 0)),
        out_shape=jax.ShapeDtypeStruct((b, s, d), F32),
        scratch_shapes=[
            pltpu.VMEM((s // kc, tq, kc), F32),
            pltpu.VMEM((rep * tq, 1), F32),
            pltpu.VMEM((rep * tq, 1), F32),
            pltpu.VMEM((rep * tq, ATT_HEAD_DIM), F32),
            pltpu.VMEM((tq, ATT_HEADS * ATT_HEAD_DIM), BF16),
        ],
        compiler_params=pltpu.CompilerParams(
            dimension_semantics=("arbitrary", "arbitrary"), vmem_limit_bytes=VMEM_LIMIT),
        name="dsa",
    )(q, qi, wi, k, v, ki, h, w_out.astype(BF16))


def kernel(x, positions, ffn1_norm, ffn1_wi, ffn1_wo, mix_norm, ffn2_norm, ffn2_wi, ffn2_wo, even_w_in, gla_gate_w, gla_gate_b, gla_out_norm, pool_w, pool_scale, even_w_out, odd_w_in, odd_w_out, final_norm):
    b, s, d = x.shape
    depth = ffn1_wi.shape[0]
    h = x
    for li in range(depth):
        h = _ffn(h.reshape(b * s, d), ffn1_norm[li], ffn1_wi[li], ffn1_wo[li], final_norm, final=False)
        h = h.reshape(b, s, d)
        j = li // 2
        if li % 2 == 0:
            h = _even_mixer(h, mix_norm[li], even_w_in[j], gla_gate_w[j], gla_gate_b[j], gla_out_norm[j],
                            pool_w[j], pool_scale[j], even_w_out[j])
        else:
            q, k, v, qi, ki, wi = _odd_proj(h, mix_norm[li], odd_w_in[j], positions)
            h = _dsa(h, q, k, v, qi, ki, wi, odd_w_out[j])
        last = li == depth - 1
        h = _ffn(h.reshape(b * s, d), ffn2_norm[li], ffn2_wi[li], ffn2_wo[li], final_norm, final=last)
        h = h.reshape(b, s, d)
    return h
```

```python
import functools
import math

import jax
import jax.numpy as jnp
import numpy as np
from jax import lax
from jax.experimental import pallas as pl
from jax.experimental.pallas import tpu as pltpu

F32 = jnp.float32
BF16 = jnp.bfloat16

RMS_EPS = 1e-6
ROPE_THETA = 10000.0

GLA_HEADS = 4
GLA_DK = 64
GLA_DV = 128
GLA_GATE_RANK = 16
GLA_GATE_TAU = 16.0

POOL_WINDOWS = (2, 4, 8, 16)
POOL_GROUP_DIM = 128
POOL_HALO = 16

ATT_HEADS = 8
ATT_KV_HEADS = 2
ATT_HEAD_DIM = 128
IDX_HEADS = 4
IDX_DIM = 64
TOPK_MAX = 256
Q_BLOCK = 128

LANES = 128
MASKED_SCORE = -1e30
COUNT_ROWS = 64
LOG2_E = 1.4426950408889634
VMEM_LIMIT = 56 * 1024 * 1024


def _rms(x, g):
    ms = jnp.mean(x * x, axis=-1, keepdims=True)
    return x * lax.rsqrt(ms + RMS_EPS) * g


def _dot(a, b):
    return jnp.dot(a, b, preferred_element_type=F32)


def _dot_nt(a, b):
    return lax.dot_general(a, b, (((1,), (1,)), ((), ())), preferred_element_type=F32)


def _dot_tn(a, b):
    return lax.dot_general(a, b, (((0,), (0,)), ((), ())), preferred_element_type=F32)


def _const_spec(shape):
    n = len(shape)
    return pl.BlockSpec(shape, lambda *_: (0,) * n)


def _ffn_kernel(h_ref, g_ref, wg_ref, wu_ref, wo_ref, fg_ref, o_ref, xn_ref, acc_ref, *, n_chunks, final):
    x = h_ref[...]
    xn_ref[...] = _rms(x, g_ref[...]).astype(BF16)
    acc_ref[...] = jnp.zeros_like(acc_ref)

    def body(c, carry):
        xn = xn_ref[...]
        gate = _dot(xn, wg_ref[c])
        up = _dot(xn, wu_ref[c])
        act = gate * (1.0 / (1.0 + jnp.exp(-gate))) * up
        acc_ref[...] += _dot(act.astype(BF16), wo_ref[c])
        return carry

    lax.fori_loop(0, n_chunks, body, 0)
    out = x + 0.5 * acc_ref[...]
    if final:
        out = _rms(out, fg_ref[...])
    o_ref[...] = out


def _ffn(h, norm_g, wi, wo, final_g, *, final, tm=512, tf=256):
    n, d = h.shape
    f = wo.shape[0]
    n_chunks = f // tf
    wg = wi[:, :f].astype(BF16).reshape(d, n_chunks, tf).transpose(1, 0, 2)
    wu = wi[:, f:].astype(BF16).reshape(d, n_chunks, tf).transpose(1, 0, 2)
    wo3 = wo.astype(BF16).reshape(n_chunks, tf, d)
    return pl.pallas_call(
        functools.partial(_ffn_kernel, n_chunks=n_chunks, final=final),
        grid=(n // tm,),
        in_specs=[
            pl.BlockSpec((tm, d), lambda i: (i, 0)),
            _const_spec((1, d)),
            _const_spec((n_chunks, d, tf)),
            _const_spec((n_chunks, d, tf)),
            _const_spec((n_chunks, tf, d)),
            _const_spec((1, d)),
        ],
        out_specs=pl.BlockSpec((tm, d), lambda i: (i, 0)),
        out_shape=jax.ShapeDtypeStruct((n, d), F32),
        scratch_shapes=[pltpu.VMEM((tm, d), BF16), pltpu.VMEM((tm, d), F32)],
        compiler_params=pltpu.CompilerParams(
            dimension_semantics=("arbitrary",), vmem_limit_bytes=VMEM_LIMIT),
        name="ffn_final" if final else "ffn",
    )(h, norm_g.reshape(1, d), wg, wu, wo3, final_g.reshape(1, d))


def _log_sigmoid(z):
    return jnp.minimum(z, 0.0) - jnp.log1p(jnp.exp(-jnp.abs(z)))


def _even_kernel(h_ref, g_ref, wm_ref, wa_ref, gw_ref, gb_ref, on_ref, pw_ref, ps_ref, wo_ref,
                 o_ref, state_ref, ext_ref, mix_ref, *, tt):
    t_idx = pl.program_id(1)
    hk = GLA_HEADS * GLA_DK
    hv = GLA_HEADS * GLA_DV

    @pl.when(t_idx == 0)
    def _():
        state_ref[...] = jnp.zeros_like(state_ref)
        ext_ref[0:POOL_HALO, :] = jnp.zeros((POOL_HALO, ext_ref.shape[1]), F32)

    x = h_ref[0]
    hn = _rms(x, g_ref[...]).astype(BF16)
    proj = _dot(hn, wm_ref[...])
    a_lr = _dot(hn, wa_ref[...])
    z = jnp.dot(a_lr, gw_ref[...], preferred_element_type=F32, precision=lax.Precision.HIGHEST) + gb_ref[...]
    la = _log_sigmoid(z) * (1.0 / GLA_GATE_TAU)

    row = lax.broadcasted_iota(jnp.int32, (tt, tt), 0)
    col = lax.broadcasted_iota(jnp.int32, (tt, tt), 1)
    tril = jnp.where(col <= row, 1.0, 0.0).astype(F32)
    bc = jnp.dot(tril, la, preferred_element_type=F32, precision=lax.Precision.HIGHEST)

    levels = []
    s = tt
    while s >= 2:
        levels.append(s)
        s //= 2
    rowk = lax.broadcasted_iota(jnp.int32, (tt, hk), 0)
    q_scaled = []
    k_scaled = []
    lvl_masks = []
    for s in levels:
        half = s // 2
        sh = s.bit_length() - 1
        gsel = jnp.where(col == ((row >> sh) << sh) + (half - 1), 1.0, 0.0).astype(F32)
        ref_b = jnp.dot(gsel, bc, preferred_element_type=F32, precision=lax.Precision.HIGHEST)
        upper = (rowk & (s - 1)) >= half
        q_scaled.append(jnp.exp(jnp.where(upper, bc - ref_b, 0.0)))
        k_scaled.append(jnp.exp(jnp.where(upper, 0.0, ref_b - bc)))
        lvl_masks.append((((row ^ col) >> sh) == 0) & ((row & (s - 1)) >= half) & ((col & (s - 1)) < half))

    e_b = jnp.exp(bc)
    b_last = bc[tt - 1:tt, :]
    e_last = jnp.exp(b_last)
    e_dec = jnp.exp(b_last - bc)

    for hd in range(GLA_HEADS):
        ks = slice(hd * GLA_DK, (hd + 1) * GLA_DK)
        qh = proj[:, ks] * (GLA_DK ** -0.5)
        kh = proj[:, hk + hd * GLA_DK: hk + (hd + 1) * GLA_DK]
        vh = proj[:, 2 * hk + hd * GLA_DV: 2 * hk + (hd + 1) * GLA_DV]
        vh_b = vh.astype(BF16)
        scores = jnp.zeros((tt, tt), F32)
        for li in range(len(levels)):
            qs = (qh * q_scaled[li][:, ks]).astype(BF16)
            kss = (kh * k_scaled[li][:, ks]).astype(BF16)
            scores = scores + jnp.where(lvl_masks[li], _dot_nt(qs, kss), 0.0)
        diag = jnp.sum(qh * kh, axis=-1, keepdims=True)
        st = state_ref[hd]
        inter = _dot_nt((qh * e_b[:, ks]).astype(BF16), st.astype(BF16))
        o_h = inter + _dot(scores.astype(BF16), vh_b) + diag * vh
        state_ref[hd] = st * e_last[:, ks] + _dot_tn(vh_b, (kh * e_dec[:, ks]).astype(BF16))
        gh = proj[:, 2 * hk + hv + hd * GLA_DV: 2 * hk + hv + (hd + 1) * GLA_DV]
        o_n = _rms(o_h, on_ref[:, hd * GLA_DV:(hd + 1) * GLA_DV])
        mix_ref[:, hd * GLA_DV:(hd + 1) * GLA_DV] = (o_n * gh * (1.0 / (1.0 + jnp.exp(-gh)))).astype(BF16)

    u = proj[:, 2 * hk + 2 * hv:]
    ext_ref[POOL_HALO:POOL_HALO + tt, :] = u
    tpos = (t_idx * tt + lax.broadcasted_iota(jnp.int32, (tt, 1), 0) + 1).astype(F32)
    for gi, w in enumerate(POOL_WINDOWS):
        ls = slice(gi * POOL_GROUP_DIM, (gi + 1) * POOL_GROUP_DIM)
        ug = ext_ref[POOL_HALO:POOL_HALO + tt, ls]
        win = ug
        for dlt in range(1, w):
            win = win + ext_ref[POOL_HALO - dlt:POOL_HALO - dlt + tt, ls]
        pg = win / jnp.minimum(tpos, float(w)) - ug
        pg = _dot(pg.astype(BF16), pw_ref[gi]) * ps_ref[:, ls]
        mix_ref[:, hv + gi * POOL_GROUP_DIM: hv + (gi + 1) * POOL_GROUP_DIM] = pg.astype(BF16)
    ext_ref[0:POOL_HALO, :] = ext_ref[tt:tt + POOL_HALO, :]

    o_ref[0] = x + _dot(mix_ref[...], wo_ref[...])


def _even_mixer(h, norm_g, w_in, gate_w, gate_b, out_norm, pool_w, pool_scale, w_out, *, tt=256):
    b, s, d = h.shape
    hk = GLA_HEADS * GLA_DK
    hv = GLA_HEADS * GLA_DV
    pool_dim = len(POOL_WINDOWS) * POOL_GROUP_DIM
    o_a = 2 * hk + 2 * hv
    w_main = jnp.concatenate([w_in[:, :o_a], w_in[:, o_a + GLA_GATE_RANK:]], axis=1).astype(BF16)
    w_a = jnp.pad(w_in[:, o_a:o_a + GLA_GATE_RANK], ((0, 0), (0, LANES - GLA_GATE_RANK))).astype(BF16)
    gw = jnp.pad(gate_w, ((0, LANES - GLA_GATE_RANK), (0, 0)))
    n_main = w_main.shape[1]
    mix_dim = hv + pool_dim
    return pl.pallas_call(
        functools.partial(_even_kernel, tt=tt),
        grid=(b, s // tt),
        in_specs=[
            pl.BlockSpec((1, tt, d), lambda bi, ti: (bi, ti, 0)),
            _const_spec((1, d)),
            _const_spec((d, n_main)),
            _const_spec((d, LANES)),
            _const_spec((LANES, hk)),
            _const_spec((1, hk)),
            _const_spec((1, hv)),
            _const_spec((len(POOL_WINDOWS), POOL_GROUP_DIM, POOL_GROUP_DIM)),
            _const_spec((1, pool_dim)),
            _const_spec((mix_dim, d)),
        ],
        out_specs=pl.BlockSpec((1, tt, d), lambda bi, ti: (bi, ti, 0)),
        out_shape=jax.ShapeDtypeStruct((b, s, d), F32),
        scratch_shapes=[
            pltpu.VMEM((GLA_HEADS, GLA_DV, GLA_DK), F32),
            pltpu.VMEM((POOL_HALO + tt, pool_dim), F32),
            pltpu.VMEM((tt, mix_dim), BF16),
        ],
        compiler_params=pltpu.CompilerParams(
            dimension_semantics=("arbitrary", "arbitrary"), vmem_limit_bytes=VMEM_LIMIT),
        name="even_mixer",
    )(h, norm_g.reshape(1, d), w_main, w_a, gw, gate_b.reshape(1, hk), out_norm.reshape(1, hv),
      pool_w.astype(BF16), pool_scale.reshape(1, pool_dim), w_out.astype(BF16))


def _odd_proj_kernel(h_ref, g_ref, w_ref, pos_ref, inv_ref, q_ref, k_ref, v_ref, qi_ref, ki_ref, wi_ref, *, tm):
    nq = ATT_HEADS * ATT_HEAD_DIM
    nkv = ATT_KV_HEADS * ATT_HEAD_DIM
    ni = IDX_HEADS * IDX_DIM
    hn = _rms(h_ref[0], g_ref[...]).astype(BF16)
    proj = _dot(hn, w_ref[...])
    pos = pos_ref[0]
    lane = lax.broadcasted_iota(jnp.int32, (tm, LANES), 1)

    ang_a = pos * inv_ref[0:1, :]
    cos_a = jnp.cos(ang_a)
    sin_a = jnp.sin(ang_a)
    sin_a = jnp.where(lane < ATT_HEAD_DIM // 2, -sin_a, sin_a)

    def rope_att(xh):
        return xh * cos_a + pltpu.roll(xh, ATT_HEAD_DIM // 2, 1) * sin_a

    ang_i = pos * inv_ref[1:2, :]
    cos_i = jnp.cos(ang_i)
    sin_i = jnp.sin(ang_i)
    low_half = (lane % IDX_DIM) < IDX_DIM // 2

    def rope_idx(xb):
        rot = jnp.where(low_half, -pltpu.roll(xb, LANES - IDX_DIM // 2, 1), pltpu.roll(xb, IDX_DIM // 2, 1))
        return xb * cos_i + rot * sin_i

    for hd in range(ATT_HEADS):
        xh = proj[:, hd * ATT_HEAD_DIM:(hd + 1) * ATT_HEAD_DIM]
        q_ref[0, hd] = (rope_att(xh) * (LOG2_E * ATT_HEAD_DIM ** -0.5)).astype(BF16)
    for gi in range(ATT_KV_HEADS):
        xh = proj[:, nq + gi * ATT_HEAD_DIM: nq + (gi + 1) * ATT_HEAD_DIM]
        k_ref[0, :, gi * ATT_HEAD_DIM:(gi + 1) * ATT_HEAD_DIM] = rope_att(xh).astype(BF16)
    v_ref[0] = proj[:, nq + nkv: nq + 2 * nkv].astype(BF16)
    o_i = nq + 2 * nkv
    for pi in range(ni // LANES):
        xr = rope_idx(proj[:, o_i + pi * LANES: o_i + (pi + 1) * LANES]).astype(BF16)
        for sub in range(LANES // IDX_DIM):
            qi_ref[0, pi * (LANES // IDX_DIM) + sub] = xr[:, sub * IDX_DIM:(sub + 1) * IDX_DIM]
    tail = proj[:, o_i + ni: o_i + ni + LANES]
    ki_ref[0] = rope_idx(tail)[:, :IDX_DIM].astype(BF16)
    wi_ref[0] = pltpu.roll(tail, LANES - IDX_DIM, 1)


def _odd_proj(h, norm_g, w_in, positions, *, tm=512):
    b, s, d = h.shape
    n_in = w_in.shape[1]
    n_pad = -(-n_in // LANES) * LANES
    w = jnp.pad(w_in, ((0, 0), (0, n_pad - n_in))).astype(BF16)
    inv_a = ROPE_THETA ** (-jnp.arange(0, ATT_HEAD_DIM, 2, dtype=F32) / ATT_HEAD_DIM)
    inv_i = ROPE_THETA ** (-jnp.arange(0, IDX_DIM, 2, dtype=F32) / IDX_DIM)
    inv = jnp.stack([jnp.tile(inv_a, LANES // inv_a.shape[0]), jnp.tile(inv_i, LANES // inv_i.shape[0])])
    inv = jnp.pad(inv, ((0, 6), (0, 0)))
    pos = positions.astype(F32).reshape(b, s, 1)
    return pl.pallas_call(
        functools.partial(_odd_proj_kernel, tm=tm),
        grid=(b, s // tm),
        in_specs=[
            pl.BlockSpec((1, tm, d), lambda bi, ti: (bi, ti, 0)),
            _const_spec((1, d)),
            _const_spec((d, n_pad)),
            pl.BlockSpec((1, tm, 1), lambda bi, ti: (bi, ti, 0)),
            _const_spec((8, LANES)),
        ],
        out_specs=[
            pl.BlockSpec((1, ATT_HEADS, tm, ATT_HEAD_DIM), lambda bi, ti: (bi, 0, ti, 0)),
            pl.BlockSpec((1, tm, ATT_KV_HEADS * ATT_HEAD_DIM), lambda bi, ti: (bi, ti, 0)),
            pl.BlockSpec((1, tm, ATT_KV_HEADS * ATT_HEAD_DIM), lambda bi, ti: (bi, ti, 0)),
            pl.BlockSpec((1, IDX_HEADS, tm, IDX_DIM), lambda bi, ti: (bi, 0, ti, 0)),
            pl.BlockSpec((1, tm, IDX_DIM), lambda bi, ti: (bi, ti, 0)),
            pl.BlockSpec((1, tm, LANES), lambda bi, ti: (bi, ti, 0)),
        ],
        out_shape=[
            jax.ShapeDtypeStruct((b, ATT_HEADS, s, ATT_HEAD_DIM), BF16),
            jax.ShapeDtypeStruct((b, s, ATT_KV_HEADS * ATT_HEAD_DIM), BF16),
            jax.ShapeDtypeStruct((b, s, ATT_KV_HEADS * ATT_HEAD_DIM), BF16),
            jax.ShapeDtypeStruct((b, IDX_HEADS, s, IDX_DIM), BF16),
            jax.ShapeDtypeStruct((b, s, IDX_DIM), BF16),
            jax.ShapeDtypeStruct((b, s, LANES), F32),
        ],
        compiler_params=pltpu.CompilerParams(
            dimension_semantics=("arbitrary", "arbitrary"), vmem_limit_bytes=VMEM_LIMIT),
        name="odd_proj",
    )(h, norm_g.reshape(1, d), w, pos, inv)


def _ordered_to_float(o):
    bits = jnp.where(o >= 0, o, o ^ jnp.int32(0x7FFFFFFF))
    return lax.bitcast_convert_type(bits, F32)


def _dsa_kernel(q_ref, qi_ref, wi_ref, k_ref, vt_ref, ki_ref, h_ref, wo_ref, o_ref,
                sc_ref, cut_ref, acc_ref, og_ref, *, tq, kc, top_k, idx_bits):
    qb = pl.program_id(1)
    n_kc = (qb * tq + tq + kc - 1) // kc
    rep = ATT_HEADS // ATT_KV_HEADS
    q_pos = qb * tq + lax.broadcasted_iota(jnp.int32, (1, tq), 1)
    key_iota = lax.broadcasted_iota(jnp.int32, (kc, tq), 0)
    idx_scale = (IDX_DIM ** -0.5) * (IDX_HEADS ** -0.5)

    qi = qi_ref[0].reshape(IDX_HEADS * tq, IDX_DIM)
    wi = wi_ref[0]

    def score_body(c, carry):
        off = pl.multiple_of(c * kc, kc)
        lg = jnp.maximum(_dot_nt(ki_ref[0, pl.ds(off, kc), :], qi), 0.0)
        isc = lg[:, 0:tq] * wi[0:1, :]
        for hd in range(1, IDX_HEADS):
            isc = isc + lg[:, hd * tq:(hd + 1) * tq] * wi[hd:hd + 1, :]
        isc = isc * idx_scale
        sc_ref[c] = jnp.where(key_iota + off <= q_pos, isc, -jnp.inf)
        return carry

    lax.fori_loop(0, n_kc, score_body, 0)

    def count(hit):
        def body(c, acc):
            return acc + jnp.sum(hit(sc_ref[c], c * kc).reshape(kc // COUNT_ROWS, COUNT_ROWS, tq), axis=0)
        part = lax.fori_loop(0, n_kc, body, jnp.zeros((COUNT_ROWS, tq), F32))
        return jnp.sum(part, axis=0, keepdims=True)

    kf = float(top_k)
    lo0 = jnp.full((1, tq), np.int32(-2139095041))
    hi0 = jnp.full((1, tq), np.int32(2139095041))

    def bis_body(_, carry):
        lo, hi = carry
        mid = (lo >> 1) + (hi >> 1) + (lo & hi & 1)
        thr = _ordered_to_float(mid)
        ok = count(lambda xs, off: jnp.where(xs >= thr, 1.0, 0.0)) >= kf
        return jnp.where(ok, mid, lo), jnp.where(ok, hi, mid)

    lo, _ = lax.fori_loop(0, 32, bis_body, (lo0, hi0))
    thr = _ordered_to_float(lo)
    need = kf - count(lambda xs, off: jnp.where(xs > thr, 1.0, 0.0))

    n_tie = count(lambda xs, off: jnp.where(xs == thr, 1.0, 0.0))
    cut_ref[...] = jnp.full(cut_ref.shape, np.int32(2 ** 30))

    @pl.when(jnp.max(n_tie - need) > 0.0)
    def _():
        def cut_body(it, cut):
            cand = cut | (jnp.int32(1) << (jnp.int32(idx_bits - 1) - it))
            below = count(lambda xs, off: jnp.where(xs == thr, jnp.where(key_iota + off < cand, 1.0, 0.0), 0.0))
            return jnp.where(below < need, cand, cut)

        cut_ref[...] = lax.fori_loop(0, idx_bits, cut_body, jnp.zeros((1, tq), jnp.int32))

    cut = cut_ref[...]

    def mask_body(c, carry):
        off = c * kc
        xs = sc_ref[c]
        kidx = key_iota + off
        sel = jnp.where(xs > thr, 0.0, jnp.where(xs == thr, jnp.where(kidx <= cut, 0.0, MASKED_SCORE), MASKED_SCORE))
        sc_ref[c] = jnp.where(kidx <= q_pos, sel, MASKED_SCORE)
        return carry

    lax.fori_loop(0, n_kc, mask_body, 0)

    acc_ref[...] = jnp.zeros_like(acc_ref)

    def att_body(c, carry):
        off = pl.multiple_of(c * kc, kc)
        bias = sc_ref[c]
        out = []
        sts = []
        for gi in range(ATT_KV_HEADS):
            qg = q_ref[0, gi * rep:(gi + 1) * rep].reshape(rep * tq, ATT_HEAD_DIM)
            sts.append(_dot_nt(k_ref[0, pl.ds(off, kc), gi * ATT_HEAD_DIM:(gi + 1) * ATT_HEAD_DIM], qg))
        for gi in range(ATT_KV_HEADS):
            m_old, l_old = carry[gi]
            st = jnp.concatenate([sts[gi][:, r * tq:(r + 1) * tq] + bias for r in range(rep)], axis=1)
            m_new = jnp.maximum(m_old, jnp.max(st, axis=0, keepdims=True))
            alpha = jnp.exp2(m_old - m_new)
            p = jnp.exp2(st - m_new)
            l_new = alpha * l_old + jnp.sum(p, axis=0, keepdims=True)
            acc_ref[gi] = alpha * acc_ref[gi] + _dot(vt_ref[0, gi, c], p.astype(BF16))
            out.append((m_new, l_new))
        return tuple(out)

    init = (jnp.full((1, rep * tq), MASKED_SCORE, F32), jnp.zeros((1, rep * tq), F32))
    fin = lax.fori_loop(0, n_kc, att_body, (init,) * ATT_KV_HEADS)
    for gi in range(ATT_KV_HEADS):
        og = (acc_ref[gi] / fin[gi][1]).astype(BF16)
        for r in range(rep):
            hd = gi * rep + r
            og_ref[hd * ATT_HEAD_DIM:(hd + 1) * ATT_HEAD_DIM, :] = og[:, r * tq:(r + 1) * tq]

    o_ref[0] = h_ref[0] + _dot_tn(og_ref[...], wo_ref[...])


def _dsa(h, q, k, v, qi, ki, wi, w_out, *, kc=512):
    b, s, d = h.shape
    tq = Q_BLOCK
    top_k = min(TOPK_MAX, s // 4)
    kc = min(kc, s)
    assert kc >= top_k and s % kc == 0 and kc % tq == 0
    rep = ATT_HEADS // ATT_KV_HEADS
    nkv = ATT_KV_HEADS * ATT_HEAD_DIM
    n_chunks = s // kc
    wi_t = wi[:, :, :8].transpose(0, 2, 1)
    v_t = v.reshape(b, n_chunks, kc, ATT_KV_HEADS, ATT_HEAD_DIM).transpose(0, 3, 1, 4, 2)
    return pl.pallas_call(
        functools.partial(_dsa_kernel, tq=tq, kc=kc, top_k=top_k, idx_bits=(s - 1).bit_length()),
        grid=(b, s // tq),
        in_specs=[
            pl.BlockSpec((1, ATT_HEADS, tq, ATT_HEAD_DIM), lambda bi, qi_: (bi, 0, qi_, 0)),
            pl.BlockSpec((1, IDX_HEADS, tq, IDX_DIM), lambda bi, qi_: (bi, 0, qi_, 0)),
            pl.BlockSpec((1, 8, tq), lambda bi, qi_: (bi, 0, qi_)),
            pl.BlockSpec((1, s, nkv), lambda bi, qi_: (bi, 0, 0)),
            pl.BlockSpec((1, ATT_KV_HEADS, n_chunks, ATT_HEAD_DIM, kc), lambda bi, qi_: (bi, 0, 0, 0, 0)),
            pl.BlockSpec((1, s, IDX_DIM), lambda bi, qi_: (bi, 0, 0)),
            pl.BlockSpec((1, tq, d), lambda bi, qi_: (bi, qi_, 0)),
            _const_spec((ATT_HEADS * ATT_HEAD_DIM, d)),
        ],
        out_specs=pl.BlockSpec((1, tq, d), lambda bi, qi_: (bi, qi_, 0)),
        out_shape=jax.ShapeDtypeStruct((b, s, d), F32),
        scratch_shapes=[
            pltpu.VMEM((n_chunks, kc, tq), F32),
            pltpu.VMEM((1, tq), jnp.int32),
            pltpu.VMEM((ATT_KV_HEADS, ATT_HEAD_DIM, rep * tq), F32),
            pltpu.VMEM((ATT_HEADS * ATT_HEAD_DIM, tq), BF16),
        ],
        compiler_params=pltpu.CompilerParams(
            dimension_semantics=("arbitrary", "arbitrary"), vmem_limit_bytes=VMEM_LIMIT),
        name="dsa",
    )(q, qi, wi_t, k, v_t, ki, h, w_out.astype(BF16))


def kernel(x, positions, ffn1_norm, ffn1_wi, ffn1_wo, mix_norm, ffn2_norm, ffn2_wi, ffn2_wo, even_w_in, gla_gate_w, gla_gate_b, gla_out_norm, pool_w, pool_scale, even_w_out, odd_w_in, odd_w_out, final_norm):
    b, s, d = x.shape
    depth = ffn1_wi.shape[0]
    h = x
    for li in range(depth):
        h = _ffn(h.reshape(b * s, d), ffn1_norm[li], ffn1_wi[li], ffn1_wo[li], final_norm, final=False)
        h = h.reshape(b, s, d)
        j = li // 2
        if li % 2 == 0:
            h = _even_mixer(h, mix_norm[li], even_w_in[j], gla_gate_w[j], gla_gate_b[j], gla_out_norm[j],
                            pool_w[j], pool_scale[j], even_w_out[j])
        else:
            q, k, v, qi, ki, wi = _odd_proj(h, mix_norm[li], odd_w_in[j], positions)
            h = _dsa(h, q, k, v, qi, ki, wi, odd_w_out[j])
        last = li == depth - 1
        h = _ffn(h.reshape(b * s, d), ffn2_norm[li], ffn2_wi[li], ffn2_wo[li], final_norm, final=last)
        h = h.reshape(b, s, d)
    return h
```

```python
import functools
import math

import jax
import jax.numpy as jnp
import numpy as np
from jax import lax
from jax.experimental import pallas as pl
from jax.experimental.pallas import tpu as pltpu

F32 = jnp.float32
BF16 = jnp.bfloat16

RMS_EPS = 1e-6
ROPE_THETA = 10000.0

GLA_HEADS = 4
GLA_DK = 64
GLA_DV = 128
GLA_GATE_RANK = 16
GLA_GATE_TAU = 16.0

POOL_WINDOWS = (2, 4, 8, 16)
POOL_GROUP_DIM = 128
POOL_HALO = 16

ATT_HEADS = 8
ATT_KV_HEADS = 2
ATT_HEAD_DIM = 128
IDX_HEADS = 4
IDX_DIM = 64
TOPK_MAX = 256
Q_BLOCK = 128

LANES = 128
BF16_SUBLANES = 16
MASKED_SCORE = -1e30
COUNT_ROWS = 64
BISECT_STEPS = 16
LOG2_E = 1.4426950408889634
VMEM_LIMIT = 56 * 1024 * 1024


def _rms(x, g):
    ms = jnp.mean(x * x, axis=-1, keepdims=True)
    return x * lax.rsqrt(ms + RMS_EPS) * g


def _dot(a, b):
    return jnp.dot(a, b, preferred_element_type=F32)


def _dot_nt(a, b):
    return lax.dot_general(a, b, (((1,), (1,)), ((), ())), preferred_element_type=F32)


def _dot_tn(a, b):
    return lax.dot_general(a, b, (((0,), (0,)), ((), ())), preferred_element_type=F32)


def _const_spec(shape):
    n = len(shape)
    return pl.BlockSpec(shape, lambda *_: (0,) * n)


def _ffn_kernel(h_ref, g_ref, wg_ref, wu_ref, wo_ref, fg_ref, o_ref, xn_ref, acc_ref, *, n_chunks, final):
    x = h_ref[...]
    xn_ref[...] = _rms(x, g_ref[...]).astype(BF16)
    acc_ref[...] = jnp.zeros_like(acc_ref)

    def body(c, carry):
        xn = xn_ref[...]
        gate = _dot(xn, wg_ref[c])
        up = _dot(xn, wu_ref[c])
        act = gate * (1.0 / (1.0 + jnp.exp(-gate))) * up
        acc_ref[...] += _dot(act.astype(BF16), wo_ref[c])
        return carry

    lax.fori_loop(0, n_chunks, body, 0, unroll=True)
    out = x + 0.5 * acc_ref[...]
    if final:
        out = _rms(out, fg_ref[...])
    o_ref[...] = out


def _ffn(h, norm_g, wi, wo, final_g, *, final, tm=512, tf=256):
    n, d = h.shape
    f = wo.shape[0]
    n_chunks = f // tf
    wg = wi[:, :f].astype(BF16).reshape(d, n_chunks, tf).transpose(1, 0, 2)
    wu = wi[:, f:].astype(BF16).reshape(d, n_chunks, tf).transpose(1, 0, 2)
    wo3 = wo.astype(BF16).reshape(n_chunks, tf, d)
    return pl.pallas_call(
        functools.partial(_ffn_kernel, n_chunks=n_chunks, final=final),
        grid=(n // tm,),
        in_specs=[
            pl.BlockSpec((tm, d), lambda i: (i, 0)),
            _const_spec((1, d)),
            _const_spec((n_chunks, d, tf)),
            _const_spec((n_chunks, d, tf)),
            _const_spec((n_chunks, tf, d)),
            _const_spec((1, d)),
        ],
        out_specs=pl.BlockSpec((tm, d), lambda i: (i, 0)),
        out_shape=jax.ShapeDtypeStruct((n, d), F32),
        scratch_shapes=[pltpu.VMEM((tm, d), BF16), pltpu.VMEM((tm, d), F32)],
        compiler_params=pltpu.CompilerParams(
            dimension_semantics=("arbitrary",), vmem_limit_bytes=VMEM_LIMIT),
        name="ffn_final" if final else "ffn",
    )(h, norm_g.reshape(1, d), wg, wu, wo3, final_g.reshape(1, d))


def _log_sigmoid(z):
    return jnp.minimum(z, 0.0) - jnp.log1p(jnp.exp(-jnp.abs(z)))


def _split_bf16(x, n):
    pieces = []
    for _ in range(n - 1):
        hi = x.astype(BF16)
        pieces.append(hi)
        x = x - hi.astype(F32)
    pieces.append(x.astype(BF16))
    return pieces


def _dot_pieces(sel, x, n):
    pieces = _split_bf16(x, n)
    out = _dot(sel, pieces[0])
    for piece in pieces[1:]:
        out = out + _dot(sel, piece)
    return out


def _gla_levels(tt):
    return [tt >> i for i in range(tt.bit_length() - 1)]


def _gla_constants(tt):
    levels = _gla_levels(tt)
    row = np.arange(tt)[:, None]
    col = np.arange(tt)[None, :]
    tril = (col <= row).astype(np.float32)
    gsel = np.concatenate([(col == (row // s) * s + s // 2 - 1).astype(np.float32) for s in levels], axis=0)
    lmask = np.stack([((row // s == col // s) & (row % s >= s // 2) & (col % s < s // 2)).astype(np.float32)
                      for s in levels])
    return jnp.asarray(tril, BF16), jnp.asarray(gsel, BF16), jnp.asarray(lmask, F32)


def _even_kernel(h_ref, g_ref, wm_ref, wa_ref, gw_ref, gb_ref, on_ref, pw_ref, ps_ref, wo_ref,
                 tril_ref, gsel_ref, lmask_ref, o_ref, state_ref, ext_ref, mix_ref, *, tt):
    t_idx = pl.program_id(1)
    hk = GLA_HEADS * GLA_DK
    hv = GLA_HEADS * GLA_DV

    @pl.when(t_idx == 0)
    def _():
        state_ref[...] = jnp.zeros_like(state_ref)
        ext_ref[0:POOL_HALO, :] = jnp.zeros((POOL_HALO, ext_ref.shape[1]), F32)

    x = h_ref[0]
    hn = _rms(x, g_ref[...]).astype(BF16)
    proj = _dot(hn, wm_ref[...])
    a_lr = _dot(hn, wa_ref[...])
    a_hi, a_lo = _split_bf16(a_lr, 2)
    z = _dot(a_hi, gw_ref[0]) + (_dot(a_hi, gw_ref[1]) + _dot(a_lo, gw_ref[0])) + gb_ref[...]
    la = _log_sigmoid(z) * (1.0 / GLA_GATE_TAU)

    bc = _dot_pieces(tril_ref[...], la, 3)

    levels = _gla_levels(tt)
    ref_all = _dot_pieces(gsel_ref[...], bc, 2)
    rowk = lax.broadcasted_iota(jnp.int32, (tt, hk), 0)
    q_scaled = []
    k_scaled = []
    for li, s in enumerate(levels):
        ref_b = ref_all[li * tt:(li + 1) * tt]
        upper = (rowk & (s - 1)) >= s // 2
        q_scaled.append(jnp.exp(jnp.where(upper, bc - ref_b, 0.0)))
        k_scaled.append(jnp.exp(jnp.where(upper, 0.0, ref_b - bc)))

    e_b = jnp.exp(bc)
    b_last = bc[tt - 1:tt, :]
    e_last = jnp.exp(b_last)
    e_dec = jnp.exp(b_last - bc)

    for hd in range(GLA_HEADS):
        ks = slice(hd * GLA_DK, (hd + 1) * GLA_DK)
        qh = proj[:, ks] * (GLA_DK ** -0.5)
        kh = proj[:, hk + hd * GLA_DK: hk + (hd + 1) * GLA_DK]
        vh = proj[:, 2 * hk + hd * GLA_DV: 2 * hk + (hd + 1) * GLA_DV]
        vh_b = vh.astype(BF16)
        scores = jnp.zeros((tt, tt), F32)
        for li in range(len(levels)):
            qs = (qh * q_scaled[li][:, ks]).astype(BF16)
            kss = (kh * k_scaled[li][:, ks]).astype(BF16)
            scores = scores + jnp.where(lmask_ref[li] > 0.0, _dot_nt(qs, kss), 0.0)
        diag = jnp.sum(qh * kh, axis=-1, keepdims=True)
        st = state_ref[hd]
        inter = _dot_nt((qh * e_b[:, ks]).astype(BF16), st.astype(BF16))
        o_h = inter + _dot(scores.astype(BF16), vh_b) + diag * vh
        state_ref[hd] = st * e_last[:, ks] + _dot_tn(vh_b, (kh * e_dec[:, ks]).astype(BF16))
        gh = proj[:, 2 * hk + hv + hd * GLA_DV: 2 * hk + hv + (hd + 1) * GLA_DV]
        o_n = _rms(o_h, on_ref[:, hd * GLA_DV:(hd + 1) * GLA_DV])
        mix_ref[:, hd * GLA_DV:(hd + 1) * GLA_DV] = (o_n * gh * (1.0 / (1.0 + jnp.exp(-gh)))).astype(BF16)

    u = proj[:, 2 * hk + 2 * hv:]
    ext_ref[POOL_HALO:POOL_HALO + tt, :] = u
    tpos = (t_idx * tt + lax.broadcasted_iota(jnp.int32, (tt, 1), 0) + 1).astype(F32)
    for gi, w in enumerate(POOL_WINDOWS):
        ls = slice(gi * POOL_GROUP_DIM, (gi + 1) * POOL_GROUP_DIM)
        ug = ext_ref[POOL_HALO:POOL_HALO + tt, ls]
        win = ug
        for dlt in range(1, w):
            win = win + ext_ref[POOL_HALO - dlt:POOL_HALO - dlt + tt, ls]
        pg = win / jnp.minimum(tpos, float(w)) - ug
        pg = _dot(pg.astype(BF16), pw_ref[gi]) * ps_ref[:, ls]
        mix_ref[:, hv + gi * POOL_GROUP_DIM: hv + (gi + 1) * POOL_GROUP_DIM] = pg.astype(BF16)
    ext_ref[0:POOL_HALO, :] = ext_ref[tt:tt + POOL_HALO, :]

    o_ref[0] = x + _dot(mix_ref[...], wo_ref[...])


def _even_mixer(h, norm_g, w_in, gate_w, gate_b, out_norm, pool_w, pool_scale, w_out, *, tt=256):
    b, s, d = h.shape
    hk = GLA_HEADS * GLA_DK
    hv = GLA_HEADS * GLA_DV
    pool_dim = len(POOL_WINDOWS) * POOL_GROUP_DIM
    o_a = 2 * hk + 2 * hv
    w_main = jnp.concatenate([w_in[:, :o_a], w_in[:, o_a + GLA_GATE_RANK:]], axis=1).astype(BF16)
    w_a = jnp.pad(w_in[:, o_a:o_a + GLA_GATE_RANK], ((0, 0), (0, LANES - GLA_GATE_RANK))).astype(BF16)
    gw = jnp.pad(gate_w, ((0, LANES - GLA_GATE_RANK), (0, 0)))
    gw_hi = gw.astype(BF16)
    gw = jnp.stack([gw_hi, (gw - gw_hi.astype(F32)).astype(BF16)])
    tril, gsel, lmask = _gla_constants(tt)
    n_levels = lmask.shape[0]
    n_main = w_main.shape[1]
    mix_dim = hv + pool_dim
    return pl.pallas_call(
        functools.partial(_even_kernel, tt=tt),
        grid=(b, s // tt),
        in_specs=[
            pl.BlockSpec((1, tt, d), lambda bi, ti: (bi, ti, 0)),
            _const_spec((1, d)),
            _const_spec((d, n_main)),
            _const_spec((d, LANES)),
            _const_spec((2, LANES, hk)),
            _const_spec((1, hk)),
            _const_spec((1, hv)),
            _const_spec((len(POOL_WINDOWS), POOL_GROUP_DIM, POOL_GROUP_DIM)),
            _const_spec((1, pool_dim)),
            _const_spec((mix_dim, d)),
            _const_spec((tt, tt)),
            _const_spec((n_levels * tt, tt)),
            _const_spec((n_levels, tt, tt)),
        ],
        out_specs=pl.BlockSpec((1, tt, d), lambda bi, ti: (bi, ti, 0)),
        out_shape=jax.ShapeDtypeStruct((b, s, d), F32),
        scratch_shapes=[
            pltpu.VMEM((GLA_HEADS, GLA_DV, GLA_DK), F32),
            pltpu.VMEM((POOL_HALO + tt, pool_dim), F32),
            pltpu.VMEM((tt, mix_dim), BF16),
        ],
        compiler_params=pltpu.CompilerParams(
            dimension_semantics=("arbitrary", "arbitrary"), vmem_limit_bytes=VMEM_LIMIT),
        name="even_mixer",
    )(h, norm_g.reshape(1, d), w_main, w_a, gw, gate_b.reshape(1, hk), out_norm.reshape(1, hv),
      pool_w.astype(BF16), pool_scale.reshape(1, pool_dim), w_out.astype(BF16), tril, gsel, lmask)


def _odd_proj_kernel(h_ref, g_ref, w_ref, pos_ref, inv_ref, q_ref, k_ref, v_ref, qi_ref, ki_ref, wi_ref, *, tm):
    nq = ATT_HEADS * ATT_HEAD_DIM
    nkv = ATT_KV_HEADS * ATT_HEAD_DIM
    ni = IDX_HEADS * IDX_DIM
    hn = _rms(h_ref[0], g_ref[...]).astype(BF16)
    proj = _dot(hn, w_ref[...])
    pos = pos_ref[0]
    lane = lax.broadcasted_iota(jnp.int32, (tm, LANES), 1)

    ang_a = pos * inv_ref[0:1, :]
    cos_a = jnp.cos(ang_a)
    sin_a = jnp.sin(ang_a)
    sin_a = jnp.where(lane < ATT_HEAD_DIM // 2, -sin_a, sin_a)

    def rope_att(xh):
        return xh * cos_a + pltpu.roll(xh, ATT_HEAD_DIM // 2, 1) * sin_a

    ang_i = pos * inv_ref[1:2, :]
    cos_i = jnp.cos(ang_i)
    sin_i = jnp.sin(ang_i)
    low_half = (lane % IDX_DIM) < IDX_DIM // 2

    def rope_idx(xb):
        rot = jnp.where(low_half, -pltpu.roll(xb, LANES - IDX_DIM // 2, 1), pltpu.roll(xb, IDX_DIM // 2, 1))
        return xb * cos_i + rot * sin_i

    for hd in range(ATT_HEADS):
        xh = proj[:, hd * ATT_HEAD_DIM:(hd + 1) * ATT_HEAD_DIM]
        q_ref[0, hd] = (rope_att(xh) * (LOG2_E * ATT_HEAD_DIM ** -0.5)).astype(BF16)
    for gi in range(ATT_KV_HEADS):
        xh = proj[:, nq + gi * ATT_HEAD_DIM: nq + (gi + 1) * ATT_HEAD_DIM]
        k_ref[0, :, gi * ATT_HEAD_DIM:(gi + 1) * ATT_HEAD_DIM] = rope_att(xh).astype(BF16)
    v_ref[0] = proj[:, nq + nkv: nq + 2 * nkv].astype(BF16)
    o_i = nq + 2 * nkv
    for pi in range(ni // LANES):
        xr = rope_idx(proj[:, o_i + pi * LANES: o_i + (pi + 1) * LANES]).astype(BF16)
        for sub in range(LANES // IDX_DIM):
            qi_ref[0, pi * (LANES // IDX_DIM) + sub] = xr[:, sub * IDX_DIM:(sub + 1) * IDX_DIM]
    tail = proj[:, o_i + ni: o_i + ni + LANES]
    ki_ref[0] = rope_idx(tail)[:, :IDX_DIM].astype(BF16)
    wi_ref[0] = pltpu.roll(tail, LANES - IDX_DIM, 1)


def _odd_proj(h, norm_g, w_in, positions, *, tm=512):
    b, s, d = h.shape
    n_in = w_in.shape[1]
    n_pad = -(-n_in // LANES) * LANES
    w = jnp.pad(w_in, ((0, 0), (0, n_pad - n_in))).astype(BF16)
    inv_a = ROPE_THETA ** (-jnp.arange(0, ATT_HEAD_DIM, 2, dtype=F32) / ATT_HEAD_DIM)
    inv_i = ROPE_THETA ** (-jnp.arange(0, IDX_DIM, 2, dtype=F32) / IDX_DIM)
    inv = jnp.stack([jnp.tile(inv_a, LANES // inv_a.shape[0]), jnp.tile(inv_i, LANES // inv_i.shape[0])])
    inv = jnp.pad(inv, ((0, 6), (0, 0)))
    pos = positions.astype(F32).reshape(b, s, 1)
    return pl.pallas_call(
        functools.partial(_odd_proj_kernel, tm=tm),
        grid=(b, s // tm),
        in_specs=[
            pl.BlockSpec((1, tm, d), lambda bi, ti: (bi, ti, 0)),
            _const_spec((1, d)),
            _const_spec((d, n_pad)),
            pl.BlockSpec((1, tm, 1), lambda bi, ti: (bi, ti, 0)),
            _const_spec((8, LANES)),
        ],
        out_specs=[
            pl.BlockSpec((1, ATT_HEADS, tm, ATT_HEAD_DIM), lambda bi, ti: (bi, 0, ti, 0)),
            pl.BlockSpec((1, tm, ATT_KV_HEADS * ATT_HEAD_DIM), lambda bi, ti: (bi, ti, 0)),
            pl.BlockSpec((1, tm, ATT_KV_HEADS * ATT_HEAD_DIM), lambda bi, ti: (bi, ti, 0)),
            pl.BlockSpec((1, IDX_HEADS, tm, IDX_DIM), lambda bi, ti: (bi, 0, ti, 0)),
            pl.BlockSpec((1, tm, IDX_DIM), lambda bi, ti: (bi, ti, 0)),
            pl.BlockSpec((1, tm, LANES), lambda bi, ti: (bi, ti, 0)),
        ],
        out_shape=[
            jax.ShapeDtypeStruct((b, ATT_HEADS, s, ATT_HEAD_DIM), BF16),
            jax.ShapeDtypeStruct((b, s, ATT_KV_HEADS * ATT_HEAD_DIM), BF16),
            jax.ShapeDtypeStruct((b, s, ATT_KV_HEADS * ATT_HEAD_DIM), BF16),
            jax.ShapeDtypeStruct((b, IDX_HEADS, s, IDX_DIM), BF16),
            jax.ShapeDtypeStruct((b, s, IDX_DIM), BF16),
            jax.ShapeDtypeStruct((b, s, LANES), F32),
        ],
        compiler_params=pltpu.CompilerParams(
            dimension_semantics=("arbitrary", "arbitrary"), vmem_limit_bytes=VMEM_LIMIT),
        name="odd_proj",
    )(h, norm_g.reshape(1, d), w, pos, inv)


def _float_to_ordered(x):
    bits = lax.bitcast_convert_type(x, jnp.int32)
    return jnp.where(bits >= 0, bits, bits ^ jnp.int32(0x7FFFFFFF))


def _ordered_to_float(o):
    bits = jnp.where(o >= 0, o, o ^ jnp.int32(0x7FFFFFFF))
    return lax.bitcast_convert_type(bits, F32)


def _dsa_kernel(q_ref, qi_ref, wi_ref, k_ref, vt_ref, ki_ref, h_ref, wo_ref, o_ref,
                sc_ref, cut_ref, bias_ref, qe_ref, acc_ref, og_ref, sta_ref, stb_ref, *, tq, kc, top_k, idx_bits):
    qb = pl.program_id(1)
    n_kc = (qb * tq + tq + kc - 1) // kc
    rep = ATT_HEADS // ATT_KV_HEADS
    q_pos = qb * tq + lax.broadcasted_iota(jnp.int32, (1, tq), 1)
    key_iota = lax.broadcasted_iota(jnp.int32, (kc, tq), 0)
    idx_scale = (IDX_DIM ** -0.5) * (IDX_HEADS ** -0.5)

    qi = qi_ref[0].reshape(IDX_HEADS * tq, IDX_DIM)
    wi = wi_ref[0]

    def fold(vals, op):
        return op(vals.reshape(kc // COUNT_ROWS, COUNT_ROWS, tq), axis=0)

    def score_body(c, carry):
        mx_p, mn_p = carry
        off = pl.multiple_of(c * kc, kc)
        lg = jnp.maximum(_dot_nt(ki_ref[0, pl.ds(off, kc), :], qi), 0.0)
        isc = lg[:, 0:tq] * wi[0:1, :]
        for hd in range(1, IDX_HEADS):
            isc = isc + lg[:, hd * tq:(hd + 1) * tq] * wi[hd:hd + 1, :]
        isc = isc * idx_scale
        adm = key_iota + off <= q_pos
        sc = jnp.where(adm, isc, -jnp.inf)
        sc_ref[c] = sc
        return (jnp.maximum(mx_p, fold(sc, jnp.max)),
                jnp.minimum(mn_p, fold(jnp.where(adm, isc, jnp.inf), jnp.min)))

    mx_p, mn_p = lax.fori_loop(0, n_kc, score_body, (jnp.full((COUNT_ROWS, tq), -jnp.inf, F32),
                                                     jnp.full((COUNT_ROWS, tq), jnp.inf, F32)))
    mx = jnp.max(mx_p, axis=0, keepdims=True) + 0.0
    mn = jnp.min(mn_p, axis=0, keepdims=True)

    def count(hit):
        def body(c, acc):
            return acc + fold(hit(sc_ref[c], c * kc), jnp.sum)
        part = lax.fori_loop(0, n_kc, body, jnp.zeros((COUNT_ROWS, tq), F32))
        return jnp.sum(part, axis=0, keepdims=True)

    def count_ge(t):
        return count(lambda xs, off: jnp.where(xs >= t, 1.0, 0.0))

    def max_below(t):
        def body(c, acc):
            xs = sc_ref[c]
            return jnp.maximum(acc, fold(jnp.where(xs < t, xs, -jnp.inf), jnp.max))
        part = lax.fori_loop(0, n_kc, body, jnp.full((COUNT_ROWS, tq), -jnp.inf, F32))
        return jnp.max(part, axis=0, keepdims=True)

    kf = float(top_k)
    n_adm = (q_pos + 1).astype(F32)
    small = n_adm <= kf
    state = (jnp.where(small, 0.0, mn),
             jnp.where(small, 1.0, _ordered_to_float(_float_to_ordered(mx) + 1)),
             n_adm,
             jnp.where(small, 1.0, 0.0),
             jnp.full((1, tq), -jnp.inf, F32),
             jnp.full((1, tq), kf, F32))

    def bis_body(_, st):
        lo, hi, c_lo, done, thr, c_thr = st
        mid = 0.5 * lo + 0.5 * hi
        c = count_ge(mid)
        active = done == 0.0
        stuck = (mid <= lo) | (mid >= hi)
        fin_stuck = active & stuck
        fin_hit = active & (~stuck) & (c == kf)
        up = active & (~stuck) & (c >= kf)
        down = active & (~stuck) & (c < kf)
        thr = jnp.where(fin_stuck, lo, jnp.where(fin_hit, mid, thr))
        c_thr = jnp.where(fin_stuck, c_lo, c_thr)
        done = jnp.where(fin_stuck | fin_hit, 1.0, done)
        return (jnp.where(up, mid, lo), jnp.where(down, mid, hi), jnp.where(up, c, c_lo), done, thr, c_thr)

    _, hi, _, done, thr, c_thr = lax.fori_loop(0, BISECT_STEPS, bis_body, state)

    def peel_body(st):
        _, hi, done, thr, c_thr = st
        v = max_below(hi)
        c = count_ge(v)
        active = done == 0.0
        ok = active & (c >= kf)
        done = jnp.where(ok, 1.0, done)
        return (jnp.sum(1.0 - done), jnp.where(active & (c < kf), v, hi), done,
                jnp.where(ok, v, thr), jnp.where(ok, c, c_thr))

    _, _, _, thr, c_thr = lax.while_loop(lambda st: st[0] > 0.0, peel_body,
                                         (jnp.sum(1.0 - done), hi, done, thr, c_thr))

    cut_ref[...] = jnp.full(cut_ref.shape, np.int32(2 ** 30))

    @pl.when(jnp.max(c_thr) > kf)
    def _():
        need = kf - count(lambda xs, off: jnp.where(xs > thr, 1.0, 0.0))

        def cut_body(it, cut):
            cand = cut | (jnp.int32(1) << (jnp.int32(idx_bits - 1) - it))
            below = count(lambda xs, off: jnp.where(xs == thr, jnp.where(key_iota + off < cand, 1.0, 0.0), 0.0))
            return jnp.where(below < need, cand, cut)

        cut_ref[...] = lax.fori_loop(0, idx_bits, cut_body, jnp.zeros((1, tq), jnp.int32))

    cut = cut_ref[...]

    def mask_body(c, carry):
        off = c * kc
        xs = sc_ref[c]
        kidx = key_iota + off
        sel = jnp.where(xs > thr, 0.0, jnp.where(xs == thr, jnp.where(kidx <= cut, 0.0, MASKED_SCORE), MASKED_SCORE))
        bias_ref[c] = jnp.where(kidx <= q_pos, sel, MASKED_SCORE).astype(BF16)
        return carry

    lax.fori_loop(0, n_kc, mask_body, 0)

    acc_ref[...] = jnp.zeros_like(acc_ref)
    ka = kc // 2
    own_col = ((lax.broadcasted_iota(jnp.int32, (rep * tq, tq), 0) & (tq - 1))
               == lax.broadcasted_iota(jnp.int32, (rep * tq, tq), 1))
    for gi in range(ATT_KV_HEADS):
        qe_ref[gi, :, 0:ATT_HEAD_DIM] = q_ref[0, gi * rep:(gi + 1) * rep].reshape(rep * tq, ATT_HEAD_DIM)
        qe_ref[gi, :, ATT_HEAD_DIM:] = jnp.where(own_col, 1.0, 0.0).astype(BF16)

    def scores_into(dst_ref, c, half):
        off = pl.multiple_of(c * kc + half * ka, ka)
        bias = bias_ref[c, half * ka:(half + 1) * ka, :]
        tops = []
        for gi in range(ATT_KV_HEADS):
            ke = jnp.concatenate([k_ref[0, pl.ds(off, ka), gi * ATT_HEAD_DIM:(gi + 1) * ATT_HEAD_DIM], bias], axis=1)
            st = _dot_nt(ke, qe_ref[gi])
            dst_ref[gi] = st
            tops.append(jnp.max(st, axis=0, keepdims=True))
        return tuple(tops)

    def softmax_pv(src_ref, c, half, m_all, tops):
        out = []
        for gi in range(ATT_KV_HEADS):
            m_new = jnp.maximum(m_all[gi], tops[gi])
            alpha = jnp.exp2(m_all[gi] - m_new)
            p = jnp.exp2(src_ref[gi] - m_new).astype(BF16)
            vt = vt_ref[0, gi, c, :, half * ka:(half + 1) * ka]
            acc_ref[gi] = alpha * acc_ref[gi] + _dot(vt, p)
            out.append(m_new)
        return tuple(out)

    def att_body(c, carry):
        m_all, tops_a = carry
        tops_b = scores_into(stb_ref, c, 1)
        m_all = softmax_pv(sta_ref, c, 0, m_all, tops_a)
        tops_a = scores_into(sta_ref, jnp.minimum(c + 1, n_kc - 1), 0)
        return softmax_pv(stb_ref, c, 1, m_all, tops_b), tops_a

    m0 = (jnp.full((1, rep * tq), MASKED_SCORE, F32),) * ATT_KV_HEADS
    lax.fori_loop(0, n_kc, att_body, (m0, scores_into(sta_ref, 0, 0)))
    for gi in range(ATT_KV_HEADS):
        acc = acc_ref[gi]
        og = (acc[0:ATT_HEAD_DIM] / acc[ATT_HEAD_DIM:ATT_HEAD_DIM + 1]).astype(BF16)
        for r in range(rep):
            hd = gi * rep + r
            og_ref[hd * ATT_HEAD_DIM:(hd + 1) * ATT_HEAD_DIM, :] = og[:, r * tq:(r + 1) * tq]

    o_ref[0] = h_ref[0] + _dot_tn(og_ref[...], wo_ref[...])


def _dsa(h, q, k, v, qi, ki, wi, w_out, *, kc=512):
    b, s, d = h.shape
    tq = Q_BLOCK
    top_k = min(TOPK_MAX, s // 4)
    kc = min(kc, s)
    assert kc >= top_k and s % kc == 0 and kc % tq == 0
    rep = ATT_HEADS // ATT_KV_HEADS
    nkv = ATT_KV_HEADS * ATT_HEAD_DIM
    n_chunks = s // kc
    wi_t = wi[:, :, :8].transpose(0, 2, 1)
    v_t = v.reshape(b, n_chunks, kc, ATT_KV_HEADS, ATT_HEAD_DIM).transpose(0, 3, 1, 4, 2)
    v_t = jnp.concatenate([v_t, jnp.ones((b, ATT_KV_HEADS, n_chunks, BF16_SUBLANES, kc), BF16)], axis=3)
    vrows = ATT_HEAD_DIM + BF16_SUBLANES
    return pl.pallas_call(
        functools.partial(_dsa_kernel, tq=tq, kc=kc, top_k=top_k, idx_bits=(s - 1).bit_length()),
        grid=(b, s // tq),
        in_specs=[
            pl.BlockSpec((1, ATT_HEADS, tq, ATT_HEAD_DIM), lambda bi, qi_: (bi, 0, qi_, 0)),
            pl.BlockSpec((1, IDX_HEADS, tq, IDX_DIM), lambda bi, qi_: (bi, 0, qi_, 0)),
            pl.BlockSpec((1, 8, tq), lambda bi, qi_: (bi, 0, qi_)),
            pl.BlockSpec((1, s, nkv), lambda bi, qi_: (bi, 0, 0)),
            pl.BlockSpec((1, ATT_KV_HEADS, n_chunks, vrows, kc), lambda bi, qi_: (bi, 0, 0, 0, 0)),
            pl.BlockSpec((1, s, IDX_DIM), lambda bi, qi_: (bi, 0, 0)),
            pl.BlockSpec((1, tq, d), lambda bi, qi_: (bi, qi_, 0)),
            _const_spec((ATT_HEADS * ATT_HEAD_DIM, d)),
        ],
        out_specs=pl.BlockSpec((1, tq, d), lambda bi, qi_: (bi, qi_, 0)),
        out_shape=jax.ShapeDtypeStruct((b, s, d), F32),
        scratch_shapes=[
            pltpu.VMEM((n_chunks, kc, tq), F32),
            pltpu.VMEM((1, tq), jnp.int32),
            pltpu.VMEM((n_chunks, kc, tq), BF16),
            pltpu.VMEM((ATT_KV_HEADS, rep * tq, 2 * ATT_HEAD_DIM), BF16),
            pltpu.VMEM((ATT_KV_HEADS, vrows, rep * tq), F32),
            pltpu.VMEM((ATT_HEADS * ATT_HEAD_DIM, tq), BF16),
            pltpu.VMEM((ATT_KV_HEADS, kc // 2, rep * tq), F32),
            pltpu.VMEM((ATT_KV_HEADS, kc // 2, rep * tq), F32),
        ],
        compiler_params=pltpu.CompilerParams(
            dimension_semantics=("arbitrary", "arbitrary"), vmem_limit_bytes=VMEM_LIMIT),
        name="dsa",
    )(q, qi, wi_t, k, v_t, ki, h, w_out.astype(BF16))


def kernel(x, positions, ffn1_norm, ffn1_wi, ffn1_wo, mix_norm, ffn2_norm, ffn2_wi, ffn2_wo, even_w_in, gla_gate_w, gla_gate_b, gla_out_norm, pool_w, pool_scale, even_w_out, odd_w_in, odd_w_out, final_norm):
    b, s, d = x.shape
    depth = ffn1_wi.shape[0]
    h = x
    for li in range(depth):
        h = _ffn(h.reshape(b * s, d), ffn1_norm[li], ffn1_wi[li], ffn1_wo[li], final_norm, final=False)
        h = h.reshape(b, s, d)
        j = li // 2
        if li % 2 == 0:
            h = _even_mixer(h, mix_norm[li], even_w_in[j], gla_gate_w[j], gla_gate_b[j], gla_out_norm[j],
                            pool_w[j], pool_scale[j], even_w_out[j])
        else:
            q, k, v, qi, ki, wi = _odd_proj(h, mix_norm[li], odd_w_in[j], positions)
            h = _dsa(h, q, k, v, qi, ki, wi, odd_w_out[j])
        last = li == depth - 1
        h = _ffn(h.reshape(b * s, d), ffn2_norm[li], ffn2_wi[li], ffn2_wo[li], final_norm, final=last)
        h = h.reshape(b, s, d)
    return h
```

```python
import functools
import math

import jax
import jax.numpy as jnp
import numpy as np
from jax import lax
from jax.experimental import pallas as pl
from jax.experimental.pallas import tpu as pltpu

F32 = jnp.float32
BF16 = jnp.bfloat16

RMS_EPS = 1e-6
ROPE_THETA = 10000.0

GLA_HEADS = 4
GLA_DK = 64
GLA_DV = 128
GLA_GATE_RANK = 16
GLA_GATE_TAU = 16.0

POOL_WINDOWS = (2, 4, 8, 16)
POOL_GROUP_DIM = 128
POOL_HALO = 16

ATT_HEADS = 8
ATT_KV_HEADS = 2
ATT_HEAD_DIM = 128
IDX_HEADS = 4
IDX_DIM = 64
TOPK_MAX = 256
Q_BLOCK = 128

LANES = 128
BF16_SUBLANES = 16
MXU_WIDTH = 256
MASKED_SCORE = -1e30
COUNT_ROWS = 64
BISECT_STEPS = 16
DSA_KEY_CHUNK = 512
LOG2_E = 1.4426950408889634
VMEM_LIMIT = 56 * 1024 * 1024


def _rms(x, g):
    ms = jnp.mean(x * x, axis=-1, keepdims=True)
    return x * lax.rsqrt(ms + RMS_EPS) * g


def _dot(a, b):
    return jnp.dot(a, b, preferred_element_type=F32)


def _dot_nt(a, b):
    return lax.dot_general(a, b, (((1,), (1,)), ((), ())), preferred_element_type=F32)


def _dot_tn(a, b):
    return lax.dot_general(a, b, (((0,), (0,)), ((), ())), preferred_element_type=F32)


def _const_spec(shape):
    n = len(shape)
    return pl.BlockSpec(shape, lambda *_: (0,) * n)


def _ffn_kernel(h_ref, g_ref, wi_ref, wo_ref, fg_ref, o_ref, xn_ref, acc_ref, *, f, tf, final):
    x = h_ref[...]
    xn_ref[...] = _rms(x, g_ref[...]).astype(BF16)
    for c in range(f // tf):
        xn = xn_ref[...]
        gate = _dot(xn, wi_ref[:, c * tf:(c + 1) * tf])
        up = _dot(xn, wi_ref[:, f + c * tf:f + (c + 1) * tf])
        act = gate * (1.0 / (1.0 + jnp.exp(-gate))) * up
        y = _dot(act.astype(BF16), wo_ref[c * tf:(c + 1) * tf, :])
        if c == 0:
            acc_ref[...] = y
        else:
            acc_ref[...] += y
    out = x + 0.5 * acc_ref[...]
    if final:
        out = _rms(out, fg_ref[...])
    o_ref[...] = out


def _ffn(h, norm_g, wi, wo, final_g, *, final, tm=512, tf=MXU_WIDTH):
    n, d = h.shape
    f = wo.shape[0]
    assert f % tf == 0 and n % tm == 0
    return pl.pallas_call(
        functools.partial(_ffn_kernel, f=f, tf=tf, final=final),
        grid=(n // tm,),
        in_specs=[
            pl.BlockSpec((tm, d), lambda i: (i, 0)),
            _const_spec((1, d)),
            _const_spec((d, 2 * f)),
            _const_spec((f, d)),
            _const_spec((1, d)),
        ],
        out_specs=pl.BlockSpec((tm, d), lambda i: (i, 0)),
        out_shape=jax.ShapeDtypeStruct((n, d), F32),
        scratch_shapes=[pltpu.VMEM((tm, d), BF16), pltpu.VMEM((tm, d), F32)],
        compiler_params=pltpu.CompilerParams(
            dimension_semantics=("arbitrary",), vmem_limit_bytes=VMEM_LIMIT),
        name="ffn_final" if final else "ffn",
    )(h, norm_g.reshape(1, d), wi.astype(BF16), wo.astype(BF16), final_g.reshape(1, d))


def _log_sigmoid(z):
    return jnp.minimum(z, 0.0) - jnp.log1p(jnp.exp(-jnp.abs(z)))


def _split_bf16(x, n):
    pieces = []
    for _ in range(n - 1):
        hi = x.astype(BF16)
        pieces.append(hi)
        x = x - hi.astype(F32)
    pieces.append(x.astype(BF16))
    return pieces


def _dot_pieces(sel, x, n):
    pieces = _split_bf16(x, n)
    out = _dot(sel, pieces[0])
    for piece in pieces[1:]:
        out = out + _dot(sel, piece)
    return out


def _gla_levels(tt):
    return [tt >> i for i in range(tt.bit_length() - 1)]


def _gla_constants(tt):
    levels = _gla_levels(tt)
    row = np.arange(tt)[:, None]
    col = np.arange(tt)[None, :]
    tril = (col <= row).astype(np.float32)
    gsel = np.concatenate([(col == (row // s) * s + s // 2 - 1).astype(np.float32) for s in levels], axis=0)
    lmask = np.stack([((row // s == col // s) & (row % s >= s // 2) & (col % s < s // 2)).astype(np.float32)
                      for s in levels])
    return jnp.asarray(tril, BF16), jnp.asarray(gsel, BF16), jnp.asarray(lmask, F32)


def _even_kernel(h_ref, g_ref, wm_ref, wa_ref, gw_ref, gb_ref, on_ref, pw_ref, ps_ref, wo_ref,
                 tril_ref, gsel_ref, lmask_ref, o_ref, state_ref, ext_ref, mix_ref, *, tt):
    t_idx = pl.program_id(1)
    hk = GLA_HEADS * GLA_DK
    hv = GLA_HEADS * GLA_DV

    @pl.when(t_idx == 0)
    def _():
        state_ref[...] = jnp.zeros_like(state_ref)
        ext_ref[0:POOL_HALO, :] = jnp.zeros((POOL_HALO, ext_ref.shape[1]), F32)

    x = h_ref[0]
    hn = _rms(x, g_ref[...]).astype(BF16)
    proj = _dot(hn, wm_ref[...])
    a_lr = _dot(hn, wa_ref[...])
    a_hi, a_lo = _split_bf16(a_lr, 2)
    z = _dot(a_hi, gw_ref[0]) + (_dot(a_hi, gw_ref[1]) + _dot(a_lo, gw_ref[0])) + gb_ref[...]
    la = _log_sigmoid(z) * (1.0 / GLA_GATE_TAU)

    bc = _dot_pieces(tril_ref[...], la, 3)

    levels = _gla_levels(tt)
    ref_all = _dot_pieces(gsel_ref[...], bc, 2)
    rowk = lax.broadcasted_iota(jnp.int32, (tt, hk), 0)
    q_scaled = []
    k_scaled = []
    for li, s in enumerate(levels):
        ref_b = ref_all[li * tt:(li + 1) * tt]
        upper = (rowk & (s - 1)) >= s // 2
        q_scaled.append(jnp.exp(jnp.where(upper, bc - ref_b, 0.0)))
        k_scaled.append(jnp.exp(jnp.where(upper, 0.0, ref_b - bc)))

    e_b = jnp.exp(bc)
    b_last = bc[tt - 1:tt, :]
    e_last = jnp.exp(b_last)
    e_dec = jnp.exp(b_last - bc)

    for hd in range(GLA_HEADS):
        ks = slice(hd * GLA_DK, (hd + 1) * GLA_DK)
        qh = proj[:, ks] * (GLA_DK ** -0.5)
        kh = proj[:, hk + hd * GLA_DK: hk + (hd + 1) * GLA_DK]
        vh = proj[:, 2 * hk + hd * GLA_DV: 2 * hk + (hd + 1) * GLA_DV]
        vh_b = vh.astype(BF16)
        scores = jnp.zeros((tt, tt), F32)
        for li in range(len(levels)):
            qs = (qh * q_scaled[li][:, ks]).astype(BF16)
            kss = (kh * k_scaled[li][:, ks]).astype(BF16)
            scores = scores + jnp.where(lmask_ref[li] > 0.0, _dot_nt(qs, kss), 0.0)
        diag = jnp.sum(qh * kh, axis=-1, keepdims=True)
        st = state_ref[hd]
        inter = _dot_nt((qh * e_b[:, ks]).astype(BF16), st.astype(BF16))
        o_h = inter + _dot(scores.astype(BF16), vh_b) + diag * vh
        state_ref[hd] = st * e_last[:, ks] + _dot_tn(vh_b, (kh * e_dec[:, ks]).astype(BF16))
        gh = proj[:, 2 * hk + hv + hd * GLA_DV: 2 * hk + hv + (hd + 1) * GLA_DV]
        o_n = _rms(o_h, on_ref[:, hd * GLA_DV:(hd + 1) * GLA_DV])
        mix_ref[:, hd * GLA_DV:(hd + 1) * GLA_DV] = (o_n * gh * (1.0 / (1.0 + jnp.exp(-gh)))).astype(BF16)

    u = proj[:, 2 * hk + 2 * hv:]
    ext_ref[POOL_HALO:POOL_HALO + tt, :] = u
    tpos = (t_idx * tt + lax.broadcasted_iota(jnp.int32, (tt, 1), 0) + 1).astype(F32)
    for gi, w in enumerate(POOL_WINDOWS):
        ls = slice(gi * POOL_GROUP_DIM, (gi + 1) * POOL_GROUP_DIM)
        ug = ext_ref[POOL_HALO:POOL_HALO + tt, ls]
        win = ug
        for dlt in range(1, w):
            win = win + ext_ref[POOL_HALO - dlt:POOL_HALO - dlt + tt, ls]
        pg = win / jnp.minimum(tpos, float(w)) - ug
        pg = _dot(pg.astype(BF16), pw_ref[gi]) * ps_ref[:, ls]
        mix_ref[:, hv + gi * POOL_GROUP_DIM: hv + (gi + 1) * POOL_GROUP_DIM] = pg.astype(BF16)
    ext_ref[0:POOL_HALO, :] = ext_ref[tt:tt + POOL_HALO, :]

    o_ref[0] = x + _dot(mix_ref[...], wo_ref[...])


def _even_mixer(h, norm_g, w_in, gate_w, gate_b, out_norm, pool_w, pool_scale, w_out, *, tt=256):
    b, s, d = h.shape
    hk = GLA_HEADS * GLA_DK
    hv = GLA_HEADS * GLA_DV
    pool_dim = len(POOL_WINDOWS) * POOL_GROUP_DIM
    o_a = 2 * hk + 2 * hv
    w_main = jnp.concatenate([w_in[:, :o_a], w_in[:, o_a + GLA_GATE_RANK:]], axis=1).astype(BF16)
    w_a = jnp.pad(w_in[:, o_a:o_a + GLA_GATE_RANK], ((0, 0), (0, LANES - GLA_GATE_RANK))).astype(BF16)
    gw = jnp.pad(gate_w, ((0, LANES - GLA_GATE_RANK), (0, 0)))
    gw_hi = gw.astype(BF16)
    gw = jnp.stack([gw_hi, (gw - gw_hi.astype(F32)).astype(BF16)])
    tril, gsel, lmask = _gla_constants(tt)
    n_levels = lmask.shape[0]
    n_main = w_main.shape[1]
    mix_dim = hv + pool_dim
    return pl.pallas_call(
        functools.partial(_even_kernel, tt=tt),
        grid=(b, s // tt),
        in_specs=[
            pl.BlockSpec((1, tt, d), lambda bi, ti: (bi, ti, 0)),
            _const_spec((1, d)),
            _const_spec((d, n_main)),
            _const_spec((d, LANES)),
            _const_spec((2, LANES, hk)),
            _const_spec((1, hk)),
            _const_spec((1, hv)),
            _const_spec((len(POOL_WINDOWS), POOL_GROUP_DIM, POOL_GROUP_DIM)),
            _const_spec((1, pool_dim)),
            _const_spec((mix_dim, d)),
            _const_spec((tt, tt)),
            _const_spec((n_levels * tt, tt)),
            _const_spec((n_levels, tt, tt)),
        ],
        out_specs=pl.BlockSpec((1, tt, d), lambda bi, ti: (bi, ti, 0)),
        out_shape=jax.ShapeDtypeStruct((b, s, d), F32),
        scratch_shapes=[
            pltpu.VMEM((GLA_HEADS, GLA_DV, GLA_DK), F32),
            pltpu.VMEM((POOL_HALO + tt, pool_dim), F32),
            pltpu.VMEM((tt, mix_dim), BF16),
        ],
        compiler_params=pltpu.CompilerParams(
            dimension_semantics=("arbitrary", "arbitrary"), vmem_limit_bytes=VMEM_LIMIT),
        name="even_mixer",
    )(h, norm_g.reshape(1, d), w_main, w_a, gw, gate_b.reshape(1, hk), out_norm.reshape(1, hv),
      pool_w.astype(BF16), pool_scale.reshape(1, pool_dim), w_out.astype(BF16), tril, gsel, lmask)


def _odd_proj_kernel(h_ref, g_ref, w_ref, pos_ref, inv_ref, q_ref, k_ref, vt_ref, qi_ref, ki_ref, wi_ref, *, tm):
    nq = ATT_HEADS * ATT_HEAD_DIM
    nkv = ATT_KV_HEADS * ATT_HEAD_DIM
    ni = IDX_HEADS * IDX_DIM
    hn = _rms(h_ref[0], g_ref[...]).astype(BF16)
    proj = _dot(hn, w_ref[...])
    pos = pos_ref[0]
    lane = lax.broadcasted_iota(jnp.int32, (tm, LANES), 1)

    half_a = ATT_HEAD_DIM // 2
    half_i = IDX_DIM // 2
    ang = pos * inv_ref[0:1, :]
    cos_p = jnp.cos(ang)
    sin_p = jnp.sin(ang)

    def att_table(t):
        return jnp.where(lane < half_a, t, pltpu.roll(t, half_a, 1))

    def idx_table(t):
        t32 = jnp.where((lane & (2 * half_i - 1)) < half_i, pltpu.roll(t, half_a, 1), pltpu.roll(t, half_a + half_i, 1))
        return jnp.where(lane < IDX_DIM, t32, pltpu.roll(t32, IDX_DIM, 1))

    cos_a = att_table(cos_p)
    sin_a = att_table(sin_p)
    sin_a = jnp.where(lane < half_a, -sin_a, sin_a)

    def rope_att(xh):
        return xh * cos_a + pltpu.roll(xh, ATT_HEAD_DIM // 2, 1) * sin_a

    cos_i = idx_table(cos_p)
    sin_i = idx_table(sin_p)
    low_half = (lane % IDX_DIM) < IDX_DIM // 2

    def rope_idx(xb):
        rot = jnp.where(low_half, -pltpu.roll(xb, LANES - IDX_DIM // 2, 1), pltpu.roll(xb, IDX_DIM // 2, 1))
        return xb * cos_i + rot * sin_i

    for hd in range(ATT_HEADS):
        xh = proj[:, hd * ATT_HEAD_DIM:(hd + 1) * ATT_HEAD_DIM]
        q_ref[0, hd] = (rope_att(xh) * (LOG2_E * ATT_HEAD_DIM ** -0.5)).astype(BF16)
    for gi in range(ATT_KV_HEADS):
        xh = proj[:, nq + gi * ATT_HEAD_DIM: nq + (gi + 1) * ATT_HEAD_DIM]
        k_ref[0, :, gi * ATT_HEAD_DIM:(gi + 1) * ATT_HEAD_DIM] = rope_att(xh).astype(BF16)
    for gi in range(ATT_KV_HEADS):
        vg = proj[:, nq + nkv + gi * ATT_HEAD_DIM: nq + nkv + (gi + 1) * ATT_HEAD_DIM]
        vt_ref[0, gi, 0] = vg.T.astype(BF16)
    o_i = nq + 2 * nkv
    for pi in range(ni // LANES):
        xr = rope_idx(proj[:, o_i + pi * LANES: o_i + (pi + 1) * LANES]).astype(BF16)
        for sub in range(LANES // IDX_DIM):
            qi_ref[0, pi * (LANES // IDX_DIM) + sub] = xr[:, sub * IDX_DIM:(sub + 1) * IDX_DIM]
    tail = proj[:, o_i + ni: o_i + ni + LANES]
    ki_ref[0] = rope_idx(tail)[:, :IDX_DIM].astype(BF16)
    wi_ref[0] = pltpu.roll(tail, LANES - IDX_DIM, 1).T[0:8, :]


def _odd_proj(h, norm_g, w_in, positions):
    b, s, d = h.shape
    tm = _dsa_key_chunk(s)
    n_in = w_in.shape[1]
    n_pad = -(-n_in // LANES) * LANES
    w = jnp.pad(w_in, ((0, 0), (0, n_pad - n_in))).astype(BF16)
    inv_a = ROPE_THETA ** (-jnp.arange(0, ATT_HEAD_DIM, 2, dtype=F32) / ATT_HEAD_DIM)
    inv_i = ROPE_THETA ** (-jnp.arange(0, IDX_DIM, 2, dtype=F32) / IDX_DIM)
    inv = jnp.concatenate([inv_a, inv_i])
    inv = jnp.pad(inv[None, :], ((0, 7), (0, LANES - inv.shape[0])))
    pos = positions.astype(F32).reshape(b, s, 1)
    return pl.pallas_call(
        functools.partial(_odd_proj_kernel, tm=tm),
        grid=(b, s // tm),
        in_specs=[
            pl.BlockSpec((1, tm, d), lambda bi, ti: (bi, ti, 0)),
            _const_spec((1, d)),
            _const_spec((d, n_pad)),
            pl.BlockSpec((1, tm, 1), lambda bi, ti: (bi, ti, 0)),
            _const_spec((8, LANES)),
        ],
        out_specs=[
            pl.BlockSpec((1, ATT_HEADS, tm, ATT_HEAD_DIM), lambda bi, ti: (bi, 0, ti, 0)),
            pl.BlockSpec((1, tm, ATT_KV_HEADS * ATT_HEAD_DIM), lambda bi, ti: (bi, ti, 0)),
            pl.BlockSpec((1, ATT_KV_HEADS, 1, ATT_HEAD_DIM, tm), lambda bi, ti: (bi, 0, ti, 0, 0)),
            pl.BlockSpec((1, IDX_HEADS, tm, IDX_DIM), lambda bi, ti: (bi, 0, ti, 0)),
            pl.BlockSpec((1, tm, IDX_DIM), lambda bi, ti: (bi, ti, 0)),
            pl.BlockSpec((1, 8, tm), lambda bi, ti: (bi, 0, ti)),
        ],
        out_shape=[
            jax.ShapeDtypeStruct((b, ATT_HEADS, s, ATT_HEAD_DIM), BF16),
            jax.ShapeDtypeStruct((b, s, ATT_KV_HEADS * ATT_HEAD_DIM), BF16),
            jax.ShapeDtypeStruct((b, ATT_KV_HEADS, s // tm, ATT_HEAD_DIM, tm), BF16),
            jax.ShapeDtypeStruct((b, IDX_HEADS, s, IDX_DIM), BF16),
            jax.ShapeDtypeStruct((b, s, IDX_DIM), BF16),
            jax.ShapeDtypeStruct((b, 8, s), F32),
        ],
        compiler_params=pltpu.CompilerParams(
            dimension_semantics=("arbitrary", "arbitrary"), vmem_limit_bytes=VMEM_LIMIT),
        name="odd_proj",
    )(h, norm_g.reshape(1, d), w, pos, inv)


def _float_to_ordered(x):
    bits = lax.bitcast_convert_type(x, jnp.int32)
    return jnp.where(bits >= 0, bits, bits ^ jnp.int32(0x7FFFFFFF))


def _ordered_to_float(o):
    bits = jnp.where(o >= 0, o, o ^ jnp.int32(0x7FFFFFFF))
    return lax.bitcast_convert_type(bits, F32)


def _dsa_kernel(q_ref, qi_ref, wi_ref, k_ref, vt_ref, ki_ref, h_ref, wo_ref, upper_ref, o_ref,
                sc_ref, bias_ref, qe_ref, acc_ref, og_ref, sta_ref, stb_ref, *, tq, kc, top_k):
    qb = pl.program_id(1)
    n_kc = (qb * tq + tq + kc - 1) // kc
    rep = ATT_HEADS // ATT_KV_HEADS
    q_pos = qb * tq + lax.broadcasted_iota(jnp.int32, (1, tq), 1)
    key_iota = lax.broadcasted_iota(jnp.int32, (kc, tq), 0)

    qi = qi_ref[0].reshape(IDX_HEADS * tq, IDX_DIM)
    wi = wi_ref[0] * ((IDX_DIM ** -0.5) * (IDX_HEADS ** -0.5))

    def fold(vals, op):
        return op(vals.reshape(kc // COUNT_ROWS, COUNT_ROWS, tq), axis=0)

    def score_body(c, carry):
        mx_p, mn_p = carry
        off = pl.multiple_of(c * kc, kc)
        lg = jnp.maximum(_dot_nt(ki_ref[0, pl.ds(off, kc), :], qi), 0.0)
        isc = lg[:, 0:tq] * wi[0:1, :]
        for hd in range(1, IDX_HEADS):
            isc = isc + lg[:, hd * tq:(hd + 1) * tq] * wi[hd:hd + 1, :]
        sc = jnp.where(key_iota + off <= q_pos, isc, -jnp.inf)
        sc_ref[c] = sc
        return jnp.maximum(mx_p, fold(sc, jnp.max)), jnp.minimum(mn_p, fold(isc, jnp.min))

    mx_p, mn_p = lax.fori_loop(0, n_kc, score_body, (jnp.full((COUNT_ROWS, tq), -jnp.inf, F32),
                                                     jnp.full((COUNT_ROWS, tq), jnp.inf, F32)))
    mx = jnp.max(mx_p, axis=0, keepdims=True) + 0.0
    mn = jnp.min(mn_p, axis=0, keepdims=True)

    def count_ge(t):
        def body(c, acc):
            return acc + fold(jnp.where(sc_ref[c] >= t, 1.0, 0.0), jnp.sum)
        part = lax.fori_loop(0, n_kc, body, jnp.zeros((COUNT_ROWS, tq), F32))
        return jnp.sum(part, axis=0, keepdims=True)

    def max_below(t):
        def body(c, acc):
            xs = sc_ref[c]
            return jnp.maximum(acc, fold(jnp.where(xs < t, xs, -jnp.inf), jnp.max))
        part = lax.fori_loop(0, n_kc, body, jnp.full((COUNT_ROWS, tq), -jnp.inf, F32))
        return jnp.max(part, axis=0, keepdims=True)

    kf = float(top_k)
    n_adm = (q_pos + 1).astype(F32)
    small = n_adm <= kf
    state = (jnp.where(small, 0.0, mn),
             jnp.where(small, 1.0, _ordered_to_float(_float_to_ordered(mx) + 1)),
             n_adm,
             jnp.where(small, 1.0, 0.0),
             jnp.full((1, tq), -jnp.inf, F32),
             jnp.full((1, tq), kf, F32))

    def bis_body(_, st):
        lo, hi, c_lo, done, thr, c_thr = st
        mid = 0.5 * lo + 0.5 * hi
        c = count_ge(mid)
        active = done == 0.0
        stuck = (mid <= lo) | (mid >= hi)
        fin_stuck = active & stuck
        fin_hit = active & (~stuck) & (c == kf)
        up = active & (~stuck) & (c >= kf)
        down = active & (~stuck) & (c < kf)
        thr = jnp.where(fin_stuck, lo, jnp.where(fin_hit, mid, thr))
        c_thr = jnp.where(fin_stuck, c_lo, c_thr)
        done = jnp.where(fin_stuck | fin_hit, 1.0, done)
        return (jnp.where(up, mid, lo), jnp.where(down, mid, hi), jnp.where(up, c, c_lo), done, thr, c_thr)

    _, hi, _, done, thr, c_thr = lax.fori_loop(0, BISECT_STEPS, bis_body, state)

    def peel_body(st):
        _, hi, done, thr, c_thr = st
        v = max_below(hi)
        c = count_ge(v)
        active = done == 0.0
        ok = active & (c >= kf)
        done = jnp.where(ok, 1.0, done)
        return (jnp.sum(1.0 - done), jnp.where(active & (c < kf), v, hi), done,
                jnp.where(ok, v, thr), jnp.where(ok, c, c_thr))

    _, _, _, thr, c_thr = lax.while_loop(lambda st: st[0] > 0.0, peel_body,
                                         (jnp.sum(1.0 - done), hi, done, thr, c_thr))

    ka = kc // 2
    excess = c_thr - kf
    half_iota = lax.broadcasted_iota(jnp.int32, (ka, tq), 0)

    def mask_body(j, later_total):
        c = n_kc - 1 - j
        halves = []
        for half in (1, 0):
            xs = sc_ref[c, half * ka:(half + 1) * ka, :]
            ties = jnp.where(xs == thr, 1.0, 0.0)
            halves.append((half, xs, ties, _dot(upper_ref[...], ties.astype(BF16))))
        for half, xs, ties, later_in_half in halves:
            later = later_in_half + later_total
            sel = jnp.where(xs > thr, 0.0,
                            jnp.where(ties > 0.0, jnp.where(later >= excess, 0.0, MASKED_SCORE), MASKED_SCORE))
            kidx = half_iota + (c * kc + half * ka)
            bias_ref[c, half * ka:(half + 1) * ka, :] = jnp.where(kidx <= q_pos, sel, MASKED_SCORE).astype(BF16)
            later_total = later[0:1, :] + ties[0:1, :]
        return later_total

    lax.fori_loop(0, n_kc, mask_body, jnp.zeros((1, tq), F32))

    acc_ref[...] = jnp.zeros_like(acc_ref)
    ones_rows = jnp.ones((BF16_SUBLANES, ka), BF16)
    own_col = ((lax.broadcasted_iota(jnp.int32, (rep * tq, tq), 0) & (tq - 1))
               == lax.broadcasted_iota(jnp.int32, (rep * tq, tq), 1))
    for gi in range(ATT_KV_HEADS):
        qe_ref[gi, :, 0:ATT_HEAD_DIM] = q_ref[0, gi * rep:(gi + 1) * rep].reshape(rep * tq, ATT_HEAD_DIM)
        qe_ref[gi, :, ATT_HEAD_DIM:] = jnp.where(own_col, 1.0, 0.0).astype(BF16)

    def scores_into(dst_ref, c, half):
        off = pl.multiple_of(c * kc + half * ka, ka)
        bias = bias_ref[c, half * ka:(half + 1) * ka, :]
        tops = []
        for gi in range(ATT_KV_HEADS):
            ke = jnp.concatenate([k_ref[0, pl.ds(off, ka), gi * ATT_HEAD_DIM:(gi + 1) * ATT_HEAD_DIM], bias], axis=1)
            st = _dot_nt(ke, qe_ref[gi])
            dst_ref[gi] = st
            tops.append(jnp.max(st, axis=0, keepdims=True))
        return tuple(tops)

    def softmax_pv(src_ref, c, half, m_all, tops):
        out = []
        for gi in range(ATT_KV_HEADS):
            m_new = jnp.maximum(m_all[gi], tops[gi])
            alpha = jnp.exp2(m_all[gi] - m_new)
            p = jnp.exp2(src_ref[gi] - m_new).astype(BF16)
            vt = jnp.concatenate([vt_ref[0, gi, c, :, half * ka:(half + 1) * ka], ones_rows], axis=0)
            acc_ref[gi] = alpha * acc_ref[gi] + _dot(vt, p)
            out.append(m_new)
        return tuple(out)

    def att_body(c, carry):
        m_all, tops_a = carry
        tops_b = scores_into(stb_ref, c, 1)
        m_all = softmax_pv(sta_ref, c, 0, m_all, tops_a)
        tops_a = scores_into(sta_ref, jnp.minimum(c + 1, n_kc - 1), 0)
        return softmax_pv(stb_ref, c, 1, m_all, tops_b), tops_a

    m0 = (jnp.full((1, rep * tq), MASKED_SCORE, F32),) * ATT_KV_HEADS
    lax.fori_loop(0, n_kc, att_body, (m0, scores_into(sta_ref, 0, 0)))
    for gi in range(ATT_KV_HEADS):
        acc = acc_ref[gi]
        og = (acc[0:ATT_HEAD_DIM] / acc[ATT_HEAD_DIM:ATT_HEAD_DIM + 1]).astype(BF16)
        for r in range(rep):
            hd = gi * rep + r
            og_ref[hd * ATT_HEAD_DIM:(hd + 1) * ATT_HEAD_DIM, :] = og[:, r * tq:(r + 1) * tq]

    o_ref[0] = h_ref[0] + _dot_tn(og_ref[...], wo_ref[...])


def _dsa_key_chunk(s):
    return min(DSA_KEY_CHUNK, s)


def _dsa(h, q, k, v_t, qi, ki, wi_t, w_out):
    b, s, d = h.shape
    tq = Q_BLOCK
    top_k = min(TOPK_MAX, s // 4)
    kc = _dsa_key_chunk(s)
    assert kc >= top_k and s % kc == 0 and kc % (2 * tq) == 0
    rep = ATT_HEADS // ATT_KV_HEADS
    nkv = ATT_KV_HEADS * ATT_HEAD_DIM
    n_chunks = s // kc
    vrows = ATT_HEAD_DIM + BF16_SUBLANES
    ka = kc // 2
    upper = jnp.asarray(np.arange(ka)[None, :] > np.arange(ka)[:, None], BF16)
    return pl.pallas_call(
        functools.partial(_dsa_kernel, tq=tq, kc=kc, top_k=top_k),
        grid=(b, s // tq),
        in_specs=[
            pl.BlockSpec((1, ATT_HEADS, tq, ATT_HEAD_DIM), lambda bi, qi_: (bi, 0, qi_, 0)),
            pl.BlockSpec((1, IDX_HEADS, tq, IDX_DIM), lambda bi, qi_: (bi, 0, qi_, 0)),
            pl.BlockSpec((1, 8, tq), lambda bi, qi_: (bi, 0, qi_)),
            pl.BlockSpec((1, s, nkv), lambda bi, qi_: (bi, 0, 0)),
            pl.BlockSpec((1, ATT_KV_HEADS, n_chunks, ATT_HEAD_DIM, kc), lambda bi, qi_: (bi, 0, 0, 0, 0)),
            pl.BlockSpec((1, s, IDX_DIM), lambda bi, qi_: (bi, 0, 0)),
            pl.BlockSpec((1, tq, d), lambda bi, qi_: (bi, qi_, 0)),
            _const_spec((ATT_HEADS * ATT_HEAD_DIM, d)),
            _const_spec((ka, ka)),
        ],
        out_specs=pl.BlockSpec((1, tq, d), lambda bi, qi_: (bi, qi_, 0)),
        out_shape=jax.ShapeDtypeStruct((b, s, d), F32),
        scratch_shapes=[
            pltpu.VMEM((n_chunks, kc, tq), F32),
            pltpu.VMEM((n_chunks, kc, tq), BF16),
            pltpu.VMEM((ATT_KV_HEADS, rep * tq, 2 * ATT_HEAD_DIM), BF16),
            pltpu.VMEM((ATT_KV_HEADS, vrows, rep * tq), F32),
            pltpu.VMEM((ATT_HEADS * ATT_HEAD_DIM, tq), BF16),
            pltpu.VMEM((ATT_KV_HEADS, kc // 2, rep * tq), F32),
            pltpu.VMEM((ATT_KV_HEADS, kc // 2, rep * tq), F32),
        ],
        compiler_params=pltpu.CompilerParams(
            dimension_semantics=("arbitrary", "arbitrary"), vmem_limit_bytes=VMEM_LIMIT),
        name="dsa",
    )(q, qi, wi_t, k, v_t, ki, h, w_out.astype(BF16), upper)


def kernel(x, positions, ffn1_norm, ffn1_wi, ffn1_wo, mix_norm, ffn2_norm, ffn2_wi, ffn2_wo, even_w_in, gla_gate_w, gla_gate_b, gla_out_norm, pool_w, pool_scale, even_w_out, odd_w_in, odd_w_out, final_norm):
    b, s, d = x.shape
    depth = ffn1_wi.shape[0]
    h = x
    for li in range(depth):
        h = _ffn(h.reshape(b * s, d), ffn1_norm[li], ffn1_wi[li], ffn1_wo[li], final_norm, final=False)
        h = h.reshape(b, s, d)
        j = li // 2
        if li % 2 == 0:
            h = _even_mixer(h, mix_norm[li], even_w_in[j], gla_gate_w[j], gla_gate_b[j], gla_out_norm[j],
                            pool_w[j], pool_scale[j], even_w_out[j])
        else:
            q, k, v_t, qi, ki, wi_t = _odd_proj(h, mix_norm[li], odd_w_in[j], positions)
            h = _dsa(h, q, k, v_t, qi, ki, wi_t, odd_w_out[j])
        last = li == depth - 1
        h = _ffn(h.reshape(b * s, d), ffn2_norm[li], ffn2_wi[li], ffn2_wo[li], final_norm, final=last)
        h = h.reshape(b, s, d)
    return h
```

```python
import functools
import math

import jax
import jax.numpy as jnp
import numpy as np
from jax import lax
from jax.experimental import pallas as pl
from jax.experimental.pallas import tpu as pltpu

F32 = jnp.float32
BF16 = jnp.bfloat16

RMS_EPS = 1e-6
ROPE_THETA = 10000.0

GLA_HEADS = 4
GLA_DK = 64
GLA_DV = 128
GLA_GATE_RANK = 16
GLA_GATE_TAU = 16.0

POOL_WINDOWS = (2, 4, 8, 16)
POOL_GROUP_DIM = 128
POOL_HALO = 16

ATT_HEADS = 8
ATT_KV_HEADS = 2
ATT_HEAD_DIM = 128
IDX_HEADS = 4
IDX_DIM = 64
TOPK_MAX = 256
Q_BLOCK = 128

LANES = 128
BF16_SUBLANES = 16
MXU_WIDTH = 256
MASKED_SCORE = -1e30
COUNT_ROWS = 64
BISECT_STEPS = 16
DSA_KEY_CHUNK = 512
LOG2_E = 1.4426950408889634
VMEM_LIMIT = 56 * 1024 * 1024


def _rms(x, g):
    ms = jnp.mean(x * x, axis=-1, keepdims=True)
    return x * lax.rsqrt(ms + RMS_EPS) * g


def _dot(a, b):
    return jnp.dot(a, b, preferred_element_type=F32)


def _dot_nt(a, b):
    return lax.dot_general(a, b, (((1,), (1,)), ((), ())), preferred_element_type=F32)


def _dot_tn(a, b):
    return lax.dot_general(a, b, (((0,), (0,)), ((), ())), preferred_element_type=F32)


def _const_spec(shape):
    n = len(shape)
    return pl.BlockSpec(shape, lambda *_: (0,) * n)


def _ffn_kernel(h_ref, g_ref, wi_ref, wo_ref, fg_ref, o_ref, xn_ref, acc_ref, *, f, tf, final):
    x = h_ref[...]
    xn_ref[...] = _rms(x, g_ref[...]).astype(BF16)
    for c in range(f // tf):
        xn = xn_ref[...]
        gate = _dot(xn, wi_ref[0, :, c * tf:(c + 1) * tf])
        up = _dot(xn, wi_ref[0, :, f + c * tf:f + (c + 1) * tf])
        act = gate * (1.0 / (1.0 + jnp.exp(-gate))) * up
        y = _dot(act.astype(BF16), wo_ref[0, c * tf:(c + 1) * tf, :])
        if c == 0:
            acc_ref[...] = y
        else:
            acc_ref[...] += y
    out = x + 0.5 * acc_ref[...]
    if final:
        out = _rms(out, fg_ref[...])
    o_ref[...] = out


def _ffn(h, norm_g, wi_all, wo_all, layer, final_g, *, final, tm=512, tf=MXU_WIDTH):
    n, d = h.shape
    f = wo_all.shape[1]
    assert f % tf == 0 and n % tm == 0
    return pl.pallas_call(
        functools.partial(_ffn_kernel, f=f, tf=tf, final=final),
        grid=(n // tm,),
        in_specs=[
            pl.BlockSpec((tm, d), lambda i: (i, 0)),
            _const_spec((1, d)),
            pl.BlockSpec((1, d, 2 * f), lambda i: (layer, 0, 0)),
            pl.BlockSpec((1, f, d), lambda i: (layer, 0, 0)),
            _const_spec((1, d)),
        ],
        out_specs=pl.BlockSpec((tm, d), lambda i: (i, 0)),
        out_shape=jax.ShapeDtypeStruct((n, d), F32),
        scratch_shapes=[pltpu.VMEM((tm, d), BF16), pltpu.VMEM((tm, d), F32)],
        compiler_params=pltpu.CompilerParams(
            dimension_semantics=("arbitrary",), vmem_limit_bytes=VMEM_LIMIT),
        name="ffn_final" if final else "ffn",
    )(h, norm_g.reshape(1, d), wi_all, wo_all, final_g.reshape(1, d))


def _log_sigmoid(z):
    return jnp.minimum(z, 0.0) - jnp.log1p(jnp.exp(-jnp.abs(z)))


def _split_bf16(x, n):
    pieces = []
    for _ in range(n - 1):
        hi = x.astype(BF16)
        pieces.append(hi)
        x = x - hi.astype(F32)
    pieces.append(x.astype(BF16))
    return pieces


def _dot_pieces(sel, x, n):
    pieces = _split_bf16(x, n)
    out = _dot(sel, pieces[0])
    for piece in pieces[1:]:
        out = out + _dot(sel, piece)
    return out


def _gla_levels(tt):
    return [tt >> i for i in range(tt.bit_length() - 1)]


def _gla_constants(tt):
    levels = _gla_levels(tt)
    row = np.arange(tt)[:, None]
    col = np.arange(tt)[None, :]
    tril = (col <= row).astype(np.float32)
    gsel = np.concatenate([(col == (row // s) * s + s // 2 - 1).astype(np.float32) for s in levels], axis=0)
    lmask = np.stack([((row // s == col // s) & (row % s >= s // 2) & (col % s < s // 2)).astype(np.float32)
                      for s in levels])
    return jnp.asarray(tril, BF16), jnp.asarray(gsel, BF16), jnp.asarray(lmask, F32)


def _even_kernel(h_ref, g_ref, wm_ref, wa_ref, gw_ref, gb_ref, on_ref, pw_ref, ps_ref, wo_ref,
                 tril_ref, gsel_ref, lmask_ref, o_ref, state_ref, ext_ref, mix_ref, *, tt):
    t_idx = pl.program_id(1)
    hk = GLA_HEADS * GLA_DK
    hv = GLA_HEADS * GLA_DV

    @pl.when(t_idx == 0)
    def _():
        state_ref[...] = jnp.zeros_like(state_ref)
        ext_ref[0:POOL_HALO, :] = jnp.zeros((POOL_HALO, ext_ref.shape[1]), F32)

    x = h_ref[0]
    hn = _rms(x, g_ref[...]).astype(BF16)
    a_lr = _dot(hn, wa_ref[...])
    a_hi, a_lo = _split_bf16(a_lr, 2)
    z = _dot(a_hi, gw_ref[0]) + (_dot(a_hi, gw_ref[1]) + _dot(a_lo, gw_ref[0])) + gb_ref[...]
    la = _log_sigmoid(z) * (1.0 / GLA_GATE_TAU)
    proj_u = _dot(hn, wm_ref[:, 2 * hk + 2 * hv:])

    bc = _dot_pieces(tril_ref[...], la, 3)
    proj_v = _dot(hn, wm_ref[:, 2 * hk:2 * hk + hv])

    levels = _gla_levels(tt)
    ref_all = _dot_pieces(gsel_ref[...], bc, 2)

    ext_ref[POOL_HALO:POOL_HALO + tt, :] = proj_u
    tpos = (t_idx * tt + lax.broadcasted_iota(jnp.int32, (tt, 1), 0) + 1).astype(F32)
    for gi, w in enumerate(POOL_WINDOWS):
        ls = slice(gi * POOL_GROUP_DIM, (gi + 1) * POOL_GROUP_DIM)
        ug = ext_ref[POOL_HALO:POOL_HALO + tt, ls]
        win = ug
        for dlt in range(1, w):
            win = win + ext_ref[POOL_HALO - dlt:POOL_HALO - dlt + tt, ls]
        pg = win / jnp.minimum(tpos, float(w)) - ug
        pg = _dot(pg.astype(BF16), pw_ref[gi]) * ps_ref[:, ls]
        mix_ref[:, hv + gi * POOL_GROUP_DIM: hv + (gi + 1) * POOL_GROUP_DIM] = pg.astype(BF16)
    ext_ref[0:POOL_HALO, :] = ext_ref[tt:tt + POOL_HALO, :]

    proj_qk = _dot(hn, wm_ref[:, 0:2 * hk])
    proj_g = _dot(hn, wm_ref[:, 2 * hk + hv:2 * hk + 2 * hv])
    rowk = lax.broadcasted_iota(jnp.int32, (tt, hk), 0)
    q_scaled = []
    k_scaled = []
    for li, s in enumerate(levels):
        ref_b = ref_all[li * tt:(li + 1) * tt]
        upper = (rowk & (s - 1)) >= s // 2
        q_scaled.append(jnp.exp(jnp.where(upper, bc - ref_b, 0.0)))
        k_scaled.append(jnp.exp(jnp.where(upper, 0.0, ref_b - bc)))

    e_b = jnp.exp(bc)
    b_last = bc[tt - 1:tt, :]
    e_last = jnp.exp(b_last)
    e_dec = jnp.exp(b_last - bc)

    for hd in range(GLA_HEADS):
        ks = slice(hd * GLA_DK, (hd + 1) * GLA_DK)
        qh = proj_qk[:, ks] * (GLA_DK ** -0.5)
        kh = proj_qk[:, hk + hd * GLA_DK: hk + (hd + 1) * GLA_DK]
        vh = proj_v[:, hd * GLA_DV:(hd + 1) * GLA_DV]
        vh_b = vh.astype(BF16)
        scores = jnp.zeros((tt, tt), F32)
        for li in range(len(levels)):
            qs = (qh * q_scaled[li][:, ks]).astype(BF16)
            kss = (kh * k_scaled[li][:, ks]).astype(BF16)
            scores = scores + jnp.where(lmask_ref[li] > 0.0, _dot_nt(qs, kss), 0.0)
        diag = jnp.sum(qh * kh, axis=-1, keepdims=True)
        st = state_ref[hd]
        inter = _dot_nt((qh * e_b[:, ks]).astype(BF16), st.astype(BF16))
        o_h = inter + _dot(scores.astype(BF16), vh_b) + diag * vh
        state_ref[hd] = st * e_last[:, ks] + _dot_tn(vh_b, (kh * e_dec[:, ks]).astype(BF16))
        gh = proj_g[:, hd * GLA_DV:(hd + 1) * GLA_DV]
        o_n = _rms(o_h, on_ref[:, hd * GLA_DV:(hd + 1) * GLA_DV])
        mix_ref[:, hd * GLA_DV:(hd + 1) * GLA_DV] = (o_n * gh * (1.0 / (1.0 + jnp.exp(-gh)))).astype(BF16)

    o_ref[0] = x + _dot(mix_ref[...], wo_ref[...])


def _even_mixer(h, norm_g, w_in, gate_w, gate_b, out_norm, pool_w, pool_scale, w_out, *, tt=256):
    b, s, d = h.shape
    hk = GLA_HEADS * GLA_DK
    hv = GLA_HEADS * GLA_DV
    pool_dim = len(POOL_WINDOWS) * POOL_GROUP_DIM
    o_a = 2 * hk + 2 * hv
    w_main = jnp.concatenate([w_in[:, :o_a], w_in[:, o_a + GLA_GATE_RANK:]], axis=1).astype(BF16)
    w_a = jnp.pad(w_in[:, o_a:o_a + GLA_GATE_RANK], ((0, 0), (0, LANES - GLA_GATE_RANK))).astype(BF16)
    gw = jnp.pad(gate_w, ((0, LANES - GLA_GATE_RANK), (0, 0)))
    gw_hi = gw.astype(BF16)
    gw = jnp.stack([gw_hi, (gw - gw_hi.astype(F32)).astype(BF16)])
    tril, gsel, lmask = _gla_constants(tt)
    n_levels = lmask.shape[0]
    n_main = w_main.shape[1]
    mix_dim = hv + pool_dim
    return pl.pallas_call(
        functools.partial(_even_kernel, tt=tt),
        grid=(b, s // tt),
        in_specs=[
            pl.BlockSpec((1, tt, d), lambda bi, ti: (bi, ti, 0)),
            _const_spec((1, d)),
            _const_spec((d, n_main)),
            _const_spec((d, LANES)),
            _const_spec((2, LANES, hk)),
            _const_spec((1, hk)),
            _const_spec((1, hv)),
            _const_spec((len(POOL_WINDOWS), POOL_GROUP_DIM, POOL_GROUP_DIM)),
            _const_spec((1, pool_dim)),
            _const_spec((mix_dim, d)),
            _const_spec((tt, tt)),
            _const_spec((n_levels * tt, tt)),
            _const_spec((n_levels, tt, tt)),
        ],
        out_specs=pl.BlockSpec((1, tt, d), lambda bi, ti: (bi, ti, 0)),
        out_shape=jax.ShapeDtypeStruct((b, s, d), F32),
        scratch_shapes=[
            pltpu.VMEM((GLA_HEADS, GLA_DV, GLA_DK), F32),
            pltpu.VMEM((POOL_HALO + tt, pool_dim), F32),
            pltpu.VMEM((tt, mix_dim), BF16),
        ],
        compiler_params=pltpu.CompilerParams(
            dimension_semantics=("arbitrary", "arbitrary"), vmem_limit_bytes=VMEM_LIMIT),
        name="even_mixer",
    )(h, norm_g.reshape(1, d), w_main, w_a, gw, gate_b.reshape(1, hk), out_norm.reshape(1, hv),
      pool_w.astype(BF16), pool_scale.reshape(1, pool_dim), w_out.astype(BF16), tril, gsel, lmask)


def _odd_proj_kernel(h_ref, g_ref, w_ref, pos_ref, inv_ref, q_ref, k_ref, vt_ref, qi_ref, ki_ref, wi_ref, *, tm):
    nq = ATT_HEADS * ATT_HEAD_DIM
    nkv = ATT_KV_HEADS * ATT_HEAD_DIM
    ni = IDX_HEADS * IDX_DIM
    hn = _rms(h_ref[0], g_ref[...]).astype(BF16)

    def proj_cols(lo, width):
        return _dot(hn, w_ref[:, lo:lo + width])

    per_dot = MXU_WIDTH // ATT_HEAD_DIM
    o_i = nq + 2 * nkv
    xv = proj_cols(nq + nkv, nkv)
    for gi in range(ATT_KV_HEADS):
        vt_ref[0, gi, 0] = xv[:, gi * ATT_HEAD_DIM:(gi + 1) * ATT_HEAD_DIM].T.astype(BF16)
    xqs = [proj_cols(h0 * ATT_HEAD_DIM, MXU_WIDTH) for h0 in range(0, ATT_HEADS, per_dot)]
    xk = proj_cols(nq, nkv)
    xi = proj_cols(o_i, ni)
    tail = proj_cols(o_i + ni, LANES)
    wi_ref[0] = pltpu.roll(tail, LANES - IDX_DIM, 1).T[0:8, :]

    pos = pos_ref[0]
    lane = lax.broadcasted_iota(jnp.int32, (tm, LANES), 1)

    half_a = ATT_HEAD_DIM // 2
    half_i = IDX_DIM // 2
    ang = pos * inv_ref[0:1, :]
    cos_p = jnp.cos(ang)
    sin_p = jnp.sin(ang)

    def att_table(t):
        return jnp.where(lane < half_a, t, pltpu.roll(t, half_a, 1))

    def idx_table(t):
        t32 = jnp.where((lane & (2 * half_i - 1)) < half_i, pltpu.roll(t, half_a, 1), pltpu.roll(t, half_a + half_i, 1))
        return jnp.where(lane < IDX_DIM, t32, pltpu.roll(t32, IDX_DIM, 1))

    cos_a = att_table(cos_p)
    sin_a = att_table(sin_p)
    sin_a = jnp.where(lane < half_a, -sin_a, sin_a)

    def rope_att(xh):
        return xh * cos_a + pltpu.roll(xh, ATT_HEAD_DIM // 2, 1) * sin_a

    cos_i = idx_table(cos_p)
    sin_i = idx_table(sin_p)
    low_half = (lane % IDX_DIM) < IDX_DIM // 2

    def rope_idx(xb):
        rot = jnp.where(low_half, -pltpu.roll(xb, LANES - IDX_DIM // 2, 1), pltpu.roll(xb, IDX_DIM // 2, 1))
        return xb * cos_i + rot * sin_i

    for g, xq in enumerate(xqs):
        for hd in range(per_dot):
            xh = xq[:, hd * ATT_HEAD_DIM:(hd + 1) * ATT_HEAD_DIM]
            q_ref[0, g * per_dot + hd] = (rope_att(xh) * (LOG2_E * ATT_HEAD_DIM ** -0.5)).astype(BF16)
    for gi in range(ATT_KV_HEADS):
        xh = xk[:, gi * ATT_HEAD_DIM:(gi + 1) * ATT_HEAD_DIM]
        k_ref[0, :, gi * ATT_HEAD_DIM:(gi + 1) * ATT_HEAD_DIM] = rope_att(xh).astype(BF16)
    for pi in range(ni // LANES):
        xr = rope_idx(xi[:, pi * LANES:(pi + 1) * LANES]).astype(BF16)
        for sub in range(LANES // IDX_DIM):
            qi_ref[0, pi * (LANES // IDX_DIM) + sub] = xr[:, sub * IDX_DIM:(sub + 1) * IDX_DIM]
    ki_ref[0] = rope_idx(tail)[:, :IDX_DIM].astype(BF16)


def _odd_proj(h, norm_g, w_in, positions):
    b, s, d = h.shape
    tm = _dsa_key_chunk(s)
    n_in = w_in.shape[1]
    n_pad = -(-n_in // LANES) * LANES
    w = jnp.pad(w_in, ((0, 0), (0, n_pad - n_in))).astype(BF16)
    inv_a = ROPE_THETA ** (-jnp.arange(0, ATT_HEAD_DIM, 2, dtype=F32) / ATT_HEAD_DIM)
    inv_i = ROPE_THETA ** (-jnp.arange(0, IDX_DIM, 2, dtype=F32) / IDX_DIM)
    inv = jnp.concatenate([inv_a, inv_i])
    inv = jnp.pad(inv[None, :], ((0, 7), (0, LANES - inv.shape[0])))
    pos = positions.astype(F32).reshape(b, s, 1)
    return pl.pallas_call(
        functools.partial(_odd_proj_kernel, tm=tm),
        grid=(b, s // tm),
        in_specs=[
            pl.BlockSpec((1, tm, d), lambda bi, ti: (bi, ti, 0)),
            _const_spec((1, d)),
            _const_spec((d, n_pad)),
            pl.BlockSpec((1, tm, 1), lambda bi, ti: (bi, ti, 0)),
            _const_spec((8, LANES)),
        ],
        out_specs=[
            pl.BlockSpec((1, ATT_HEADS, tm, ATT_HEAD_DIM), lambda bi, ti: (bi, 0, ti, 0)),
            pl.BlockSpec((1, tm, ATT_KV_HEADS * ATT_HEAD_DIM), lambda bi, ti: (bi, ti, 0)),
            pl.BlockSpec((1, ATT_KV_HEADS, 1, ATT_HEAD_DIM, tm), lambda bi, ti: (bi, 0, ti, 0, 0)),
            pl.BlockSpec((1, IDX_HEADS, tm, IDX_DIM), lambda bi, ti: (bi, 0, ti, 0)),
            pl.BlockSpec((1, tm, IDX_DIM), lambda bi, ti: (bi, ti, 0)),
            pl.BlockSpec((1, 8, tm), lambda bi, ti: (bi, 0, ti)),
        ],
        out_shape=[
            jax.ShapeDtypeStruct((b, ATT_HEADS, s, ATT_HEAD_DIM), BF16),
            jax.ShapeDtypeStruct((b, s, ATT_KV_HEADS * ATT_HEAD_DIM), BF16),
            jax.ShapeDtypeStruct((b, ATT_KV_HEADS, s // tm, ATT_HEAD_DIM, tm), BF16),
            jax.ShapeDtypeStruct((b, IDX_HEADS, s, IDX_DIM), BF16),
            jax.ShapeDtypeStruct((b, s, IDX_DIM), BF16),
            jax.ShapeDtypeStruct((b, 8, s), F32),
        ],
        compiler_params=pltpu.CompilerParams(
            dimension_semantics=("arbitrary", "arbitrary"), vmem_limit_bytes=VMEM_LIMIT),
        name="odd_proj",
    )(h, norm_g.reshape(1, d), w, pos, inv)


def _float_to_ordered(x):
    bits = lax.bitcast_convert_type(x, jnp.int32)
    return jnp.where(bits >= 0, bits, bits ^ jnp.int32(0x7FFFFFFF))


def _ordered_to_float(o):
    bits = jnp.where(o >= 0, o, o ^ jnp.int32(0x7FFFFFFF))
    return lax.bitcast_convert_type(bits, F32)


def _dsa_kernel(q_ref, qi_ref, wi_ref, k_ref, vt_ref, ki_ref, h_ref, wo_ref, upper_ref, o_ref,
                sc_ref, bias_ref, qe_ref, acc_ref, og_ref, sta_ref, stb_ref, *, tq, kc, top_k):
    qb = pl.program_id(1)
    n_kc = (qb * tq + tq + kc - 1) // kc
    rep = ATT_HEADS // ATT_KV_HEADS
    q_pos = qb * tq + lax.broadcasted_iota(jnp.int32, (1, tq), 1)
    key_iota = lax.broadcasted_iota(jnp.int32, (kc, tq), 0)

    qi = qi_ref[0].reshape(IDX_HEADS * tq, IDX_DIM)
    wi = wi_ref[0] * ((IDX_DIM ** -0.5) * (IDX_HEADS ** -0.5))

    def fold(vals, op):
        return op(vals.reshape(kc // COUNT_ROWS, COUNT_ROWS, tq), axis=0)

    def score_body(c, carry):
        mx_p, mn_p = carry
        off = pl.multiple_of(c * kc, kc)
        lg = jnp.maximum(_dot_nt(ki_ref[0, pl.ds(off, kc), :], qi), 0.0)
        isc = lg[:, 0:tq] * wi[0:1, :]
        for hd in range(1, IDX_HEADS):
            isc = isc + lg[:, hd * tq:(hd + 1) * tq] * wi[hd:hd + 1, :]
        sc = jnp.where(key_iota + off <= q_pos, isc, -jnp.inf)
        sc_ref[c] = sc
        return jnp.maximum(mx_p, fold(sc, jnp.max)), jnp.minimum(mn_p, fold(isc, jnp.min))

    mx_p, mn_p = lax.fori_loop(0, n_kc, score_body, (jnp.full((COUNT_ROWS, tq), -jnp.inf, F32),
                                                     jnp.full((COUNT_ROWS, tq), jnp.inf, F32)))
    mx = jnp.max(mx_p, axis=0, keepdims=True) + 0.0
    mn = jnp.min(mn_p, axis=0, keepdims=True)

    def count_ge(t):
        def body(c, acc):
            return acc + fold(jnp.where(sc_ref[c] >= t, 1.0, 0.0), jnp.sum)
        part = lax.fori_loop(0, n_kc, body, jnp.zeros((COUNT_ROWS, tq), F32))
        return jnp.sum(part, axis=0, keepdims=True)

    def max_below(t):
        def body(c, acc):
            xs = sc_ref[c]
            return jnp.maximum(acc, fold(jnp.where(xs < t, xs, -jnp.inf), jnp.max))
        part = lax.fori_loop(0, n_kc, body, jnp.full((COUNT_ROWS, tq), -jnp.inf, F32))
        return jnp.max(part, axis=0, keepdims=True)

    kf = float(top_k)
    n_adm = (q_pos + 1).astype(F32)
    small = n_adm <= kf
    state = (jnp.where(small, 0.0, mn),
             jnp.where(small, 1.0, _ordered_to_float(_float_to_ordered(mx) + 1)),
             n_adm,
             jnp.where(small, 1.0, 0.0),
             jnp.full((1, tq), -jnp.inf, F32),
             jnp.full((1, tq), kf, F32))

    def bis_body(_, st):
        lo, hi, c_lo, done, thr, c_thr = st
        mid = 0.5 * lo + 0.5 * hi
        c = count_ge(mid)
        active = done == 0.0
        stuck = (mid <= lo) | (mid >= hi)
        fin_stuck = active & stuck
        fin_hit = active & (~stuck) & (c == kf)
        up = active & (~stuck) & (c >= kf)
        down = active & (~stuck) & (c < kf)
        thr = jnp.where(fin_stuck, lo, jnp.where(fin_hit, mid, thr))
        c_thr = jnp.where(fin_stuck, c_lo, c_thr)
        done = jnp.where(fin_stuck | fin_hit, 1.0, done)
        return (jnp.where(up, mid, lo), jnp.where(down, mid, hi), jnp.where(up, c, c_lo), done, thr, c_thr)

    _, hi, _, done, thr, c_thr = lax.fori_loop(0, BISECT_STEPS, bis_body, state)

    def peel_body(st):
        _, hi, done, thr, c_thr = st
        v = max_below(hi)
        c = count_ge(v)
        active = done == 0.0
        ok = active & (c >= kf)
        done = jnp.where(ok, 1.0, done)
        return (jnp.sum(1.0 - done), jnp.where(active & (c < kf), v, hi), done,
                jnp.where(ok, v, thr), jnp.where(ok, c, c_thr))

    _, _, _, thr, c_thr = lax.while_loop(lambda st: st[0] > 0.0, peel_body,
                                         (jnp.sum(1.0 - done), hi, done, thr, c_thr))

    ka = kc // 2
    excess = c_thr - kf
    half_iota = lax.broadcasted_iota(jnp.int32, (ka, tq), 0)

    def mask_body(j, later_total):
        c = n_kc - 1 - j
        halves = []
        for half in (1, 0):
            xs = sc_ref[c, half * ka:(half + 1) * ka, :]
            ties = jnp.where(xs == thr, 1.0, 0.0)
            halves.append((half, xs, ties, _dot(upper_ref[...], ties.astype(BF16))))
        for half, xs, ties, later_in_half in halves:
            later = later_in_half + later_total
            sel = jnp.where(xs > thr, 0.0,
                            jnp.where(ties > 0.0, jnp.where(later >= excess, 0.0, MASKED_SCORE), MASKED_SCORE))
            kidx = half_iota + (c * kc + half * ka)
            bias_ref[c, half * ka:(half + 1) * ka, :] = jnp.where(kidx <= q_pos, sel, MASKED_SCORE).astype(BF16)
            later_total = later[0:1, :] + ties[0:1, :]
        return later_total

    lax.fori_loop(0, n_kc, mask_body, jnp.zeros((1, tq), F32))

    acc_ref[...] = jnp.zeros_like(acc_ref)
    ones_rows = jnp.ones((BF16_SUBLANES, ka), BF16)
    own_col = ((lax.broadcasted_iota(jnp.int32, (rep * tq, tq), 0) & (tq - 1))
               == lax.broadcasted_iota(jnp.int32, (rep * tq, tq), 1))
    for gi in range(ATT_KV_HEADS):
        qe_ref[gi, :, 0:ATT_HEAD_DIM] = q_ref[0, gi * rep:(gi + 1) * rep].reshape(rep * tq, ATT_HEAD_DIM)
        qe_ref[gi, :, ATT_HEAD_DIM:] = jnp.where(own_col, 1.0, 0.0).astype(BF16)

    def scores_into(dst_ref, c, half):
        off = pl.multiple_of(c * kc + half * ka, ka)
        bias = bias_ref[c, half * ka:(half + 1) * ka, :]
        tops = []
        for gi in range(ATT_KV_HEADS):
            ke = jnp.concatenate([k_ref[0, pl.ds(off, ka), gi * ATT_HEAD_DIM:(gi + 1) * ATT_HEAD_DIM], bias], axis=1)
            st = _dot_nt(ke, qe_ref[gi])
            dst_ref[gi] = st
            tops.append(jnp.max(st, axis=0, keepdims=True))
        return tuple(tops)

    def softmax_pv(src_ref, c, half, m_all, tops):
        out = []
        for gi in range(ATT_KV_HEADS):
            m_new = jnp.maximum(m_all[gi], tops[gi])
            alpha = jnp.exp2(m_all[gi] - m_new)
            p = jnp.exp2(src_ref[gi] - m_new).astype(BF16)
            vt = jnp.concatenate([vt_ref[0, gi, c, :, half * ka:(half + 1) * ka], ones_rows], axis=0)
            acc_ref[gi] = alpha * acc_ref[gi] + _dot(vt, p)
            out.append(m_new)
        return tuple(out)

    def att_body(c, carry):
        m_all, tops_a = carry
        tops_b = scores_into(stb_ref, c, 1)
        m_all = softmax_pv(sta_ref, c, 0, m_all, tops_a)
        tops_a = scores_into(sta_ref, jnp.minimum(c + 1, n_kc - 1), 0)
        return softmax_pv(stb_ref, c, 1, m_all, tops_b), tops_a

    m0 = (jnp.full((1, rep * tq), MASKED_SCORE, F32),) * ATT_KV_HEADS
    lax.fori_loop(0, n_kc, att_body, (m0, scores_into(sta_ref, 0, 0)))
    for gi in range(ATT_KV_HEADS):
        acc = acc_ref[gi]
        og = (acc[0:ATT_HEAD_DIM] / acc[ATT_HEAD_DIM:ATT_HEAD_DIM + 1]).astype(BF16)
        for r in range(rep):
            hd = gi * rep + r
            og_ref[hd * ATT_HEAD_DIM:(hd + 1) * ATT_HEAD_DIM, :] = og[:, r * tq:(r + 1) * tq]

    o_ref[0] = h_ref[0] + _dot_tn(og_ref[...], wo_ref[...])


def _dsa_key_chunk(s):
    return min(DSA_KEY_CHUNK, s)


def _dsa(h, q, k, v_t, qi, ki, wi_t, w_out):
    b, s, d = h.shape
    tq = Q_BLOCK
    top_k = min(TOPK_MAX, s // 4)
    kc = _dsa_key_chunk(s)
    assert kc >= top_k and s % kc == 0 and kc % (2 * tq) == 0
    rep = ATT_HEADS // ATT_KV_HEADS
    nkv = ATT_KV_HEADS * ATT_HEAD_DIM
    n_chunks = s // kc
    vrows = ATT_HEAD_DIM + BF16_SUBLANES
    ka = kc // 2
    upper = jnp.asarray(np.arange(ka)[None, :] > np.arange(ka)[:, None], BF16)
    return pl.pallas_call(
        functools.partial(_dsa_kernel, tq=tq, kc=kc, top_k=top_k),
        grid=(b, s // tq),
        in_specs=[
            pl.BlockSpec((1, ATT_HEADS, tq, ATT_HEAD_DIM), lambda bi, qi_: (bi, 0, qi_, 0)),
            pl.BlockSpec((1, IDX_HEADS, tq, IDX_DIM), lambda bi, qi_: (bi, 0, qi_, 0)),
            pl.BlockSpec((1, 8, tq), lambda bi, qi_: (bi, 0, qi_)),
            pl.BlockSpec((1, s, nkv), lambda bi, qi_: (bi, 0, 0)),
            pl.BlockSpec((1, ATT_KV_HEADS, n_chunks, ATT_HEAD_DIM, kc), lambda bi, qi_: (bi, 0, 0, 0, 0)),
            pl.BlockSpec((1, s, IDX_DIM), lambda bi, qi_: (bi, 0, 0)),
            pl.BlockSpec((1, tq, d), lambda bi, qi_: (bi, qi_, 0)),
            _const_spec((ATT_HEADS * ATT_HEAD_DIM, d)),
            _const_spec((ka, ka)),
        ],
        out_specs=pl.BlockSpec((1, tq, d), lambda bi, qi_: (bi, qi_, 0)),
        out_shape=jax.ShapeDtypeStruct((b, s, d), F32),
        scratch_shapes=[
            pltpu.VMEM((n_chunks, kc, tq), F32),
            pltpu.VMEM((n_chunks, kc, tq), BF16),
            pltpu.VMEM((ATT_KV_HEADS, rep * tq, 2 * ATT_HEAD_DIM), BF16),
            pltpu.VMEM((ATT_KV_HEADS, vrows, rep * tq), F32),
            pltpu.VMEM((ATT_HEADS * ATT_HEAD_DIM, tq), BF16),
            pltpu.VMEM((ATT_KV_HEADS, kc // 2, rep * tq), F32),
            pltpu.VMEM((ATT_KV_HEADS, kc // 2, rep * tq), F32),
        ],
        compiler_params=pltpu.CompilerParams(
            dimension_semantics=("arbitrary", "arbitrary"), vmem_limit_bytes=VMEM_LIMIT),
        name="dsa",
    )(q, qi, wi_t, k, v_t, ki, h, w_out.astype(BF16), upper)


def kernel(x, positions, ffn1_norm, ffn1_wi, ffn1_wo, mix_norm, ffn2_norm, ffn2_wi, ffn2_wo, even_w_in, gla_gate_w, gla_gate_b, gla_out_norm, pool_w, pool_scale, even_w_out, odd_w_in, odd_w_out, final_norm):
    b, s, d = x.shape
    depth = ffn1_wi.shape[0]
    h = x
    ffn1_wi, ffn1_wo, ffn2_wi, ffn2_wo = (w.astype(BF16) for w in (ffn1_wi, ffn1_wo, ffn2_wi, ffn2_wo))
    for li in range(depth):
        h = _ffn(h.reshape(b * s, d), ffn1_norm[li], ffn1_wi, ffn1_wo, li, final_norm, final=False)
        h = h.reshape(b, s, d)
        j = li // 2
        if li % 2 == 0:
            h = _even_mixer(h, mix_norm[li], even_w_in[j], gla_gate_w[j], gla_gate_b[j], gla_out_norm[j],
                            pool_w[j], pool_scale[j], even_w_out[j])
        else:
            q, k, v_t, qi, ki, wi_t = _odd_proj(h, mix_norm[li], odd_w_in[j], positions)
            h = _dsa(h, q, k, v_t, qi, ki, wi_t, odd_w_out[j])
        last = li == depth - 1
        h = _ffn(h.reshape(b * s, d), ffn2_norm[li], ffn2_wi, ffn2_wo, li, final_norm, final=last)
        h = h.reshape(b, s, d)
    return h
```

```python
import functools
import math

import jax
import jax.numpy as jnp
import numpy as np
from jax import lax
from jax.experimental import pallas as pl
from jax.experimental.pallas import tpu as pltpu

F32 = jnp.float32
BF16 = jnp.bfloat16

RMS_EPS = 1e-6
ROPE_THETA = 10000.0

GLA_HEADS = 4
GLA_DK = 64
GLA_DV = 128
GLA_GATE_RANK = 16
GLA_GATE_TAU = 16.0

POOL_WINDOWS = (2, 4, 8, 16)
POOL_GROUP_DIM = 128
POOL_HALO = 16

ATT_HEADS = 8
ATT_KV_HEADS = 2
ATT_HEAD_DIM = 128
IDX_HEADS = 4
IDX_DIM = 64
TOPK_MAX = 256
DSA_QUERY_TILE = 256

LANES = 128
BF16_SUBLANES = 16
MXU_WIDTH = 256
MASKED_SCORE = -1e30
SCAN_ROWS = 128
COUNT_ROWS = 32
BISECT_STEPS = 16
DSA_KEY_CHUNK = 512
LOG2_E = 1.4426950408889634
VMEM_LIMIT = 56 * 1024 * 1024


def _rms(x, g):
    ms = jnp.mean(x * x, axis=-1, keepdims=True)
    return x * lax.rsqrt(ms + RMS_EPS) * g


def _dot(a, b):
    return jnp.dot(a, b, preferred_element_type=F32)


def _dot_nt(a, b):
    return lax.dot_general(a, b, (((1,), (1,)), ((), ())), preferred_element_type=F32)


def _dot_tn(a, b):
    return lax.dot_general(a, b, (((0,), (0,)), ((), ())), preferred_element_type=F32)


def _const_spec(shape):
    n = len(shape)
    return pl.BlockSpec(shape, lambda *_: (0,) * n)


def _ffn_kernel(h_ref, g_ref, wi_ref, wo_ref, fg_ref, o_ref, xn_ref, acc_ref, *, f, tf, final):
    x = h_ref[...]
    xn_ref[...] = _rms(x, g_ref[...]).astype(BF16)
    for c in range(f // tf):
        xn = xn_ref[...]
        gate = _dot(xn, wi_ref[0, :, c * tf:(c + 1) * tf])
        up = _dot(xn, wi_ref[0, :, f + c * tf:f + (c + 1) * tf])
        act = gate * (1.0 / (1.0 + jnp.exp(-gate))) * up
        y = _dot(act.astype(BF16), wo_ref[0, c * tf:(c + 1) * tf, :])
        if c == 0:
            acc_ref[...] = y
        else:
            acc_ref[...] += y
    out = x + 0.5 * acc_ref[...]
    if final:
        out = _rms(out, fg_ref[...])
    o_ref[...] = out


def _ffn(h, norm_g, wi_all, wo_all, layer, final_g, *, final, tm=512, tf=MXU_WIDTH):
    n, d = h.shape
    f = wo_all.shape[1]
    assert f % tf == 0 and n % tm == 0
    return pl.pallas_call(
        functools.partial(_ffn_kernel, f=f, tf=tf, final=final),
        grid=(n // tm,),
        in_specs=[
            pl.BlockSpec((tm, d), lambda i: (i, 0)),
            _const_spec((1, d)),
            pl.BlockSpec((1, d, 2 * f), lambda i: (layer, 0, 0)),
            pl.BlockSpec((1, f, d), lambda i: (layer, 0, 0)),
            _const_spec((1, d)),
        ],
        out_specs=pl.BlockSpec((tm, d), lambda i: (i, 0)),
        out_shape=jax.ShapeDtypeStruct((n, d), F32),
        scratch_shapes=[pltpu.VMEM((tm, d), BF16), pltpu.VMEM((tm, d), F32)],
        compiler_params=pltpu.CompilerParams(
            dimension_semantics=("arbitrary",), vmem_limit_bytes=VMEM_LIMIT),
        name="ffn_final" if final else "ffn",
    )(h, norm_g.reshape(1, d), wi_all, wo_all, final_g.reshape(1, d))


def _log_sigmoid(z):
    return jnp.minimum(z, 0.0) - jnp.log1p(jnp.exp(-jnp.abs(z)))


def _split_bf16(x, n):
    pieces = []
    for _ in range(n - 1):
        hi = x.astype(BF16)
        pieces.append(hi)
        x = x - hi.astype(F32)
    pieces.append(x.astype(BF16))
    return pieces


def _dot_pieces(sel, x, n):
    pieces = _split_bf16(x, n)
    out = _dot(sel, pieces[0])
    for piece in pieces[1:]:
        out = out + _dot(sel, piece)
    return out


def _gla_levels(tt):
    return [tt >> i for i in range(tt.bit_length() - 1)]


def _gla_constants(tt):
    levels = _gla_levels(tt)
    row = np.arange(tt)[:, None]
    col = np.arange(tt)[None, :]
    tril = (col <= row).astype(np.float32)
    gsel = np.concatenate([(col == (row // s) * s + s // 2 - 1).astype(np.float32) for s in levels], axis=0)
    lmask = np.stack([((row // s == col // s) & (row % s >= s // 2) & (col % s < s // 2)).astype(np.float32)
                      for s in levels])
    return jnp.asarray(tril, BF16), jnp.asarray(gsel, BF16), jnp.asarray(lmask, F32)


def _even_kernel(h_ref, g_ref, wm_ref, wa_ref, gw_ref, gb_ref, on_ref, pw_ref, ps_ref, wo_ref,
                 tril_ref, gsel_ref, lmask_ref, o_ref, state_ref, ext_ref, mix_ref, *, tt):
    t_idx = pl.program_id(1)
    hk = GLA_HEADS * GLA_DK
    hv = GLA_HEADS * GLA_DV

    @pl.when(t_idx == 0)
    def _():
        state_ref[...] = jnp.zeros_like(state_ref)
        ext_ref[0:POOL_HALO, :] = jnp.zeros((POOL_HALO, ext_ref.shape[1]), F32)

    x = h_ref[0]
    hn = _rms(x, g_ref[...]).astype(BF16)
    a_lr = _dot(hn, wa_ref[...])
    a_hi, a_lo = _split_bf16(a_lr, 2)
    z = _dot(a_hi, gw_ref[0]) + (_dot(a_hi, gw_ref[1]) + _dot(a_lo, gw_ref[0])) + gb_ref[...]
    la = _log_sigmoid(z) * (1.0 / GLA_GATE_TAU)
    proj_u = _dot(hn, wm_ref[:, 2 * hk + 2 * hv:])

    bc = _dot_pieces(tril_ref[...], la, 3)
    proj_v = _dot(hn, wm_ref[:, 2 * hk:2 * hk + hv])

    levels = _gla_levels(tt)
    ref_all = _dot_pieces(gsel_ref[...], bc, 2)

    ext_ref[POOL_HALO:POOL_HALO + tt, :] = proj_u
    tpos = (t_idx * tt + lax.broadcasted_iota(jnp.int32, (tt, 1), 0) + 1).astype(F32)
    for gi, w in enumerate(POOL_WINDOWS):
        ls = slice(gi * POOL_GROUP_DIM, (gi + 1) * POOL_GROUP_DIM)
        ug = ext_ref[POOL_HALO:POOL_HALO + tt, ls]
        win = ug
        for dlt in range(1, w):
            win = win + ext_ref[POOL_HALO - dlt:POOL_HALO - dlt + tt, ls]
        pg = win / jnp.minimum(tpos, float(w)) - ug
        pg = _dot(pg.astype(BF16), pw_ref[gi]) * ps_ref[:, ls]
        mix_ref[:, hv + gi * POOL_GROUP_DIM: hv + (gi + 1) * POOL_GROUP_DIM] = pg.astype(BF16)
    ext_ref[0:POOL_HALO, :] = ext_ref[tt:tt + POOL_HALO, :]

    proj_qk = _dot(hn, wm_ref[:, 0:2 * hk])
    proj_g = _dot(hn, wm_ref[:, 2 * hk + hv:2 * hk + 2 * hv])
    rowk = lax.broadcasted_iota(jnp.int32, (tt, hk), 0)
    q_scaled = []
    k_scaled = []
    for li, s in enumerate(levels):
        ref_b = ref_all[li * tt:(li + 1) * tt]
        upper = (rowk & (s - 1)) >= s // 2
        q_scaled.append(jnp.exp(jnp.where(upper, bc - ref_b, 0.0)))
        k_scaled.append(jnp.exp(jnp.where(upper, 0.0, ref_b - bc)))

    e_b = jnp.exp(bc)
    b_last = bc[tt - 1:tt, :]
    e_last = jnp.exp(b_last)
    e_dec = jnp.exp(b_last - bc)

    for hd in range(GLA_HEADS):
        ks = slice(hd * GLA_DK, (hd + 1) * GLA_DK)
        qh = proj_qk[:, ks] * (GLA_DK ** -0.5)
        kh = proj_qk[:, hk + hd * GLA_DK: hk + (hd + 1) * GLA_DK]
        vh = proj_v[:, hd * GLA_DV:(hd + 1) * GLA_DV]
        vh_b = vh.astype(BF16)
        scores = jnp.zeros((tt, tt), F32)
        for li in range(len(levels)):
            qs = (qh * q_scaled[li][:, ks]).astype(BF16)
            kss = (kh * k_scaled[li][:, ks]).astype(BF16)
            scores = scores + jnp.where(lmask_ref[li] > 0.0, _dot_nt(qs, kss), 0.0)
        diag = jnp.sum(qh * kh, axis=-1, keepdims=True)
        st = state_ref[hd]
        inter = _dot_nt((qh * e_b[:, ks]).astype(BF16), st.astype(BF16))
        o_h = inter + _dot(scores.astype(BF16), vh_b) + diag * vh
        state_ref[hd] = st * e_last[:, ks] + _dot_tn(vh_b, (kh * e_dec[:, ks]).astype(BF16))
        gh = proj_g[:, hd * GLA_DV:(hd + 1) * GLA_DV]
        o_n = _rms(o_h, on_ref[:, hd * GLA_DV:(hd + 1) * GLA_DV])
        mix_ref[:, hd * GLA_DV:(hd + 1) * GLA_DV] = (o_n * gh * (1.0 / (1.0 + jnp.exp(-gh)))).astype(BF16)

    o_ref[0] = x + _dot(mix_ref[...], wo_ref[...])


def _even_mixer(h, norm_g, w_in, gate_w, gate_b, out_norm, pool_w, pool_scale, w_out, *, tt=256):
    b, s, d = h.shape
    hk = GLA_HEADS * GLA_DK
    hv = GLA_HEADS * GLA_DV
    pool_dim = len(POOL_WINDOWS) * POOL_GROUP_DIM
    o_a = 2 * hk + 2 * hv
    w_main = jnp.concatenate([w_in[:, :o_a], w_in[:, o_a + GLA_GATE_RANK:]], axis=1).astype(BF16)
    w_a = jnp.pad(w_in[:, o_a:o_a + GLA_GATE_RANK], ((0, 0), (0, LANES - GLA_GATE_RANK))).astype(BF16)
    gw = jnp.pad(gate_w, ((0, LANES - GLA_GATE_RANK), (0, 0)))
    gw_hi = gw.astype(BF16)
    gw = jnp.stack([gw_hi, (gw - gw_hi.astype(F32)).astype(BF16)])
    tril, gsel, lmask = _gla_constants(tt)
    n_levels = lmask.shape[0]
    n_main = w_main.shape[1]
    mix_dim = hv + pool_dim
    return pl.pallas_call(
        functools.partial(_even_kernel, tt=tt),
        grid=(b, s // tt),
        in_specs=[
            pl.BlockSpec((1, tt, d), lambda bi, ti: (bi, ti, 0)),
            _const_spec((1, d)),
            _const_spec((d, n_main)),
            _const_spec((d, LANES)),
            _const_spec((2, LANES, hk)),
            _const_spec((1, hk)),
            _const_spec((1, hv)),
            _const_spec((len(POOL_WINDOWS), POOL_GROUP_DIM, POOL_GROUP_DIM)),
            _const_spec((1, pool_dim)),
            _const_spec((mix_dim, d)),
            _const_spec((tt, tt)),
            _const_spec((n_levels * tt, tt)),
            _const_spec((n_levels, tt, tt)),
        ],
        out_specs=pl.BlockSpec((1, tt, d), lambda bi, ti: (bi, ti, 0)),
        out_shape=jax.ShapeDtypeStruct((b, s, d), F32),
        scratch_shapes=[
            pltpu.VMEM((GLA_HEADS, GLA_DV, GLA_DK), F32),
            pltpu.VMEM((POOL_HALO + tt, pool_dim), F32),
            pltpu.VMEM((tt, mix_dim), BF16),
        ],
        compiler_params=pltpu.CompilerParams(
            dimension_semantics=("arbitrary", "arbitrary"), vmem_limit_bytes=VMEM_LIMIT),
        name="even_mixer",
    )(h, norm_g.reshape(1, d), w_main, w_a, gw, gate_b.reshape(1, hk), out_norm.reshape(1, hv),
      pool_w.astype(BF16), pool_scale.reshape(1, pool_dim), w_out.astype(BF16), tril, gsel, lmask)


def _odd_proj_kernel(h_ref, g_ref, w_ref, pos_ref, inv_ref, q_ref, k_ref, vt_ref, qi_ref, ki_ref, wi_ref, *, tm):
    nq = ATT_HEADS * ATT_HEAD_DIM
    nkv = ATT_KV_HEADS * ATT_HEAD_DIM
    ni = IDX_HEADS * IDX_DIM
    hn = _rms(h_ref[0], g_ref[...]).astype(BF16)

    def proj_cols(lo, width):
        return _dot(hn, w_ref[:, lo:lo + width])

    per_dot = MXU_WIDTH // ATT_HEAD_DIM
    o_i = nq + 2 * nkv
    xv = proj_cols(nq + nkv, nkv)
    for gi in range(ATT_KV_HEADS):
        vt_ref[0, gi, 0] = xv[:, gi * ATT_HEAD_DIM:(gi + 1) * ATT_HEAD_DIM].T.astype(BF16)
    xqs = [proj_cols(h0 * ATT_HEAD_DIM, MXU_WIDTH) for h0 in range(0, ATT_HEADS, per_dot)]
    xk = proj_cols(nq, nkv)
    xi = proj_cols(o_i, ni)
    tail = proj_cols(o_i + ni, LANES)
    wi_ref[0] = pltpu.roll(tail, LANES - IDX_DIM, 1).T[0:8, :]

    pos = pos_ref[0]
    lane = lax.broadcasted_iota(jnp.int32, (tm, LANES), 1)

    half_a = ATT_HEAD_DIM // 2
    half_i = IDX_DIM // 2
    ang = pos * inv_ref[0:1, :]
    cos_p = jnp.cos(ang)
    sin_p = jnp.sin(ang)

    def att_table(t):
        return jnp.where(lane < half_a, t, pltpu.roll(t, half_a, 1))

    def idx_table(t):
        t32 = jnp.where((lane & (2 * half_i - 1)) < half_i, pltpu.roll(t, half_a, 1), pltpu.roll(t, half_a + half_i, 1))
        return jnp.where(lane < IDX_DIM, t32, pltpu.roll(t32, IDX_DIM, 1))

    cos_a = att_table(cos_p)
    sin_a = att_table(sin_p)
    sin_a = jnp.where(lane < half_a, -sin_a, sin_a)

    def rope_att(xh):
        return xh * cos_a + pltpu.roll(xh, ATT_HEAD_DIM // 2, 1) * sin_a

    cos_i = idx_table(cos_p)
    sin_i = idx_table(sin_p)
    low_half = (lane % IDX_DIM) < IDX_DIM // 2

    def rope_idx(xb):
        rot = jnp.where(low_half, -pltpu.roll(xb, LANES - IDX_DIM // 2, 1), pltpu.roll(xb, IDX_DIM // 2, 1))
        return xb * cos_i + rot * sin_i

    for g, xq in enumerate(xqs):
        for hd in range(per_dot):
            xh = xq[:, hd * ATT_HEAD_DIM:(hd + 1) * ATT_HEAD_DIM]
            q_ref[0, g * per_dot + hd] = (rope_att(xh) * (LOG2_E * ATT_HEAD_DIM ** -0.5)).astype(BF16)
    for gi in range(ATT_KV_HEADS):
        xh = xk[:, gi * ATT_HEAD_DIM:(gi + 1) * ATT_HEAD_DIM]
        k_ref[0, :, gi * ATT_HEAD_DIM:(gi + 1) * ATT_HEAD_DIM] = rope_att(xh).astype(BF16)
    for pi in range(ni // LANES):
        xr = rope_idx(xi[:, pi * LANES:(pi + 1) * LANES]).astype(BF16)
        for sub in range(LANES // IDX_DIM):
            qi_ref[0, pi * (LANES // IDX_DIM) + sub] = xr[:, sub * IDX_DIM:(sub + 1) * IDX_DIM]
    ki_ref[0] = rope_idx(tail)[:, :IDX_DIM].astype(BF16)


def _odd_proj(h, norm_g, w_in, positions):
    b, s, d = h.shape
    tm = _dsa_key_chunk(s)
    n_in = w_in.shape[1]
    n_pad = -(-n_in // LANES) * LANES
    w = jnp.pad(w_in, ((0, 0), (0, n_pad - n_in))).astype(BF16)
    inv_a = ROPE_THETA ** (-jnp.arange(0, ATT_HEAD_DIM, 2, dtype=F32) / ATT_HEAD_DIM)
    inv_i = ROPE_THETA ** (-jnp.arange(0, IDX_DIM, 2, dtype=F32) / IDX_DIM)
    inv = jnp.concatenate([inv_a, inv_i])
    inv = jnp.pad(inv[None, :], ((0, 7), (0, LANES - inv.shape[0])))
    pos = positions.astype(F32).reshape(b, s, 1)
    return pl.pallas_call(
        functools.partial(_odd_proj_kernel, tm=tm),
        grid=(b, s // tm),
        in_specs=[
            pl.BlockSpec((1, tm, d), lambda bi, ti: (bi, ti, 0)),
            _const_spec((1, d)),
            _const_spec((d, n_pad)),
            pl.BlockSpec((1, tm, 1), lambda bi, ti: (bi, ti, 0)),
            _const_spec((8, LANES)),
        ],
        out_specs=[
            pl.BlockSpec((1, ATT_HEADS, tm, ATT_HEAD_DIM), lambda bi, ti: (bi, 0, ti, 0)),
            pl.BlockSpec((1, tm, ATT_KV_HEADS * ATT_HEAD_DIM), lambda bi, ti: (bi, ti, 0)),
            pl.BlockSpec((1, ATT_KV_HEADS, 1, ATT_HEAD_DIM, tm), lambda bi, ti: (bi, 0, ti, 0, 0)),
            pl.BlockSpec((1, IDX_HEADS, tm, IDX_DIM), lambda bi, ti: (bi, 0, ti, 0)),
            pl.BlockSpec((1, tm, IDX_DIM), lambda bi, ti: (bi, ti, 0)),
            pl.BlockSpec((1, 8, tm), lambda bi, ti: (bi, 0, ti)),
        ],
        out_shape=[
            jax.ShapeDtypeStruct((b, ATT_HEADS, s, ATT_HEAD_DIM), BF16),
            jax.ShapeDtypeStruct((b, s, ATT_KV_HEADS * ATT_HEAD_DIM), BF16),
            jax.ShapeDtypeStruct((b, ATT_KV_HEADS, s // tm, ATT_HEAD_DIM, tm), BF16),
            jax.ShapeDtypeStruct((b, IDX_HEADS, s, IDX_DIM), BF16),
            jax.ShapeDtypeStruct((b, s, IDX_DIM), BF16),
            jax.ShapeDtypeStruct((b, 8, s), F32),
        ],
        compiler_params=pltpu.CompilerParams(
            dimension_semantics=("arbitrary", "arbitrary"), vmem_limit_bytes=VMEM_LIMIT),
        name="odd_proj",
    )(h, norm_g.reshape(1, d), w, pos, inv)


def _float_to_ordered(x):
    bits = lax.bitcast_convert_type(x, jnp.int32)
    return jnp.where(bits >= 0, bits, bits ^ jnp.int32(0x7FFFFFFF))


def _ordered_to_float(o):
    bits = jnp.where(o >= 0, o, o ^ jnp.int32(0x7FFFFFFF))
    return lax.bitcast_convert_type(bits, F32)


def _dsa_kernel(q_ref, qi_ref, wi_ref, k_ref, vt_ref, ki_ref, h_ref, wo_ref, upper_ref, o_ref,
                sc_ref, bias_ref, qe_ref, acc_ref, og_ref, sta_ref, stb_ref, pa_ref, pb_ref, *, tq, kc, top_k):
    qb = pl.program_id(1)
    n_kc = (qb * tq + tq + kc - 1) // kc
    rep = ATT_HEADS // ATT_KV_HEADS
    q_pos = qb * tq + lax.broadcasted_iota(jnp.int32, (1, tq), 1)
    scan_iota = lax.broadcasted_iota(jnp.int32, (SCAN_ROWS, tq), 0)

    qi = qi_ref[0].reshape(IDX_HEADS * tq, IDX_DIM)
    wi = wi_ref[0] * ((IDX_DIM ** -0.5) * (IDX_HEADS ** -0.5))

    def fold(vals, op):
        return op(vals.reshape(vals.shape[0] // COUNT_ROWS, COUNT_ROWS, tq), axis=0)

    def score_body(c, carry):
        mx_p, mn_p = carry
        for r0 in range(0, kc, SCAN_ROWS):
            off = pl.multiple_of(c * kc + r0, SCAN_ROWS)
            lg = jnp.maximum(_dot_nt(ki_ref[0, pl.ds(off, SCAN_ROWS), :], qi), 0.0)
            isc = lg[:, 0:tq] * wi[0:1, :]
            for hd in range(1, IDX_HEADS):
                isc = isc + lg[:, hd * tq:(hd + 1) * tq] * wi[hd:hd + 1, :]
            sc = jnp.where(scan_iota + off <= q_pos, isc, -jnp.inf)
            sc_ref[c, r0:r0 + SCAN_ROWS, :] = sc
            mx_p = jnp.maximum(mx_p, fold(sc, jnp.max))
            mn_p = jnp.minimum(mn_p, fold(isc, jnp.min))
        return mx_p, mn_p

    mx_p, mn_p = lax.fori_loop(0, n_kc, score_body, (jnp.full((COUNT_ROWS, tq), -jnp.inf, F32),
                                                     jnp.full((COUNT_ROWS, tq), jnp.inf, F32)))
    mx = jnp.max(mx_p, axis=0, keepdims=True) + 0.0
    mn = jnp.min(mn_p, axis=0, keepdims=True)

    def count_ge(t):
        def body(c, acc):
            for r0 in range(0, kc, SCAN_ROWS):
                xs = sc_ref[c, r0:r0 + SCAN_ROWS, :].reshape(SCAN_ROWS // COUNT_ROWS, COUNT_ROWS, tq)
                acc = acc + jnp.sum(jnp.where(xs >= t, 1.0, 0.0), axis=0)
            return acc
        part = lax.fori_loop(0, n_kc, body, jnp.zeros((COUNT_ROWS, tq), F32))
        return jnp.sum(part, axis=0, keepdims=True)

    def max_below(t):
        def body(c, acc):
            for r0 in range(0, kc, SCAN_ROWS):
                xs = sc_ref[c, r0:r0 + SCAN_ROWS, :]
                acc = jnp.maximum(acc, fold(jnp.where(xs < t, xs, -jnp.inf), jnp.max))
            return acc
        part = lax.fori_loop(0, n_kc, body, jnp.full((COUNT_ROWS, tq), -jnp.inf, F32))
        return jnp.max(part, axis=0, keepdims=True)

    kf = float(top_k)
    n_adm = (q_pos + 1).astype(F32)
    small = n_adm <= kf
    state = (jnp.where(small, 0.0, mn),
             jnp.where(small, 1.0, _ordered_to_float(_float_to_ordered(mx) + 1)),
             n_adm,
             jnp.where(small, 1.0, 0.0),
             jnp.full((1, tq), -jnp.inf, F32),
             jnp.full((1, tq), kf, F32))

    def bis_body(_, st):
        lo, hi, c_lo, done, thr, c_thr = st
        mid = 0.5 * lo + 0.5 * hi
        c = count_ge(mid)
        active = done == 0.0
        stuck = (mid <= lo) | (mid >= hi)
        fin_stuck = active & stuck
        fin_hit = active & (~stuck) & (c == kf)
        up = active & (~stuck) & (c >= kf)
        down = active & (~stuck) & (c < kf)
        thr = jnp.where(fin_stuck, lo, jnp.where(fin_hit, mid, thr))
        c_thr = jnp.where(fin_stuck, c_lo, c_thr)
        done = jnp.where(fin_stuck | fin_hit, 1.0, done)
        return (jnp.where(up, mid, lo), jnp.where(down, mid, hi), jnp.where(up, c, c_lo), done, thr, c_thr)

    _, hi, _, done, thr, c_thr = lax.fori_loop(0, BISECT_STEPS, bis_body, state)

    def peel_body(st):
        _, hi, done, thr, c_thr = st
        v = max_below(hi)
        c = count_ge(v)
        active = done == 0.0
        ok = active & (c >= kf)
        done = jnp.where(ok, 1.0, done)
        return (jnp.sum(1.0 - done), jnp.where(active & (c < kf), v, hi), done,
                jnp.where(ok, v, thr), jnp.where(ok, c, c_thr))

    _, _, _, thr, c_thr = lax.while_loop(lambda st: st[0] > 0.0, peel_body,
                                         (jnp.sum(1.0 - done), hi, done, thr, c_thr))

    ka = kc // 2
    excess = c_thr - kf
    half_iota = lax.broadcasted_iota(jnp.int32, (ka, tq), 0)

    def mask_body(j, later_total):
        c = n_kc - 1 - j
        halves = []
        for half in (1, 0):
            xs = sc_ref[c, half * ka:(half + 1) * ka, :]
            ties = jnp.where(xs == thr, 1.0, 0.0)
            halves.append((half, xs, ties, _dot(upper_ref[...], ties.astype(BF16))))
        for half, xs, ties, later_in_half in halves:
            later = later_in_half + later_total
            sel = jnp.where(xs > thr, 0.0,
                            jnp.where(ties > 0.0, jnp.where(later >= excess, 0.0, MASKED_SCORE), MASKED_SCORE))
            kidx = half_iota + (c * kc + half * ka)
            bias_ref[c, half * ka:(half + 1) * ka, :] = jnp.where(kidx <= q_pos, sel, MASKED_SCORE).astype(BF16)
            later_total = later[0:1, :] + ties[0:1, :]
        return later_total

    lax.fori_loop(0, n_kc, mask_body, jnp.zeros((1, tq), F32))

    acc_ref[...] = jnp.zeros_like(acc_ref)
    ones_rows = jnp.ones((BF16_SUBLANES, ka), BF16)
    n_qs = tq // LANES
    own_col = ((lax.broadcasted_iota(jnp.int32, (rep * LANES, LANES), 0) & (LANES - 1))
               == lax.broadcasted_iota(jnp.int32, (rep * LANES, LANES), 1))
    for gi in range(ATT_KV_HEADS):
        for sb in range(n_qs):
            qs = q_ref[0, gi * rep:(gi + 1) * rep, sb * LANES:(sb + 1) * LANES, :]
            qe_ref[gi, sb, :, 0:ATT_HEAD_DIM] = qs.reshape(rep * LANES, ATT_HEAD_DIM)
            qe_ref[gi, sb, :, ATT_HEAD_DIM:] = jnp.where(own_col, 1.0, 0.0).astype(BF16)

    def scores_into(dst_ref, c, half):
        off = pl.multiple_of(c * kc + half * ka, ka)
        bias = bias_ref[c, half * ka:(half + 1) * ka, :]
        tops = []
        for gi in range(ATT_KV_HEADS):
            kb = k_ref[0, pl.ds(off, ka), gi * ATT_HEAD_DIM:(gi + 1) * ATT_HEAD_DIM]
            st = jnp.concatenate(
                [_dot_nt(jnp.concatenate([kb, bias[:, sb * LANES:(sb + 1) * LANES]], axis=1), qe_ref[gi, sb])
                 for sb in range(n_qs)], axis=1)
            dst_ref[gi] = st
            tops.append(jnp.max(st, axis=0, keepdims=True))
        return tuple(tops)

    def softmax_into(p_ref, src_ref, m_all, tops):
        m_out, alphas = [], []
        for gi in range(ATT_KV_HEADS):
            m_new = jnp.maximum(m_all[gi], tops[gi])
            alphas.append(jnp.exp2(m_all[gi] - m_new))
            p_ref[gi] = jnp.exp2(src_ref[gi] - m_new).astype(BF16)
            m_out.append(m_new)
        return tuple(m_out), tuple(alphas)

    def values_in(p_ref, c, half, alphas):
        for gi in range(ATT_KV_HEADS):
            vt = jnp.concatenate([vt_ref[0, gi, c, :, half * ka:(half + 1) * ka], ones_rows], axis=0)
            acc_ref[gi] = alphas[gi] * acc_ref[gi] + _dot(vt, p_ref[gi])

    def att_body(c, carry):
        m_all, tops_a, alpha_b = carry
        tops_b = scores_into(stb_ref, c, 1)
        values_in(pb_ref, jnp.maximum(c - 1, 0), 1, alpha_b)
        m_all, alpha_a = softmax_into(pa_ref, sta_ref, m_all, tops_a)
        tops_a = scores_into(sta_ref, jnp.minimum(c + 1, n_kc - 1), 0)
        values_in(pa_ref, c, 0, alpha_a)
        m_all, alpha_b = softmax_into(pb_ref, stb_ref, m_all, tops_b)
        return m_all, tops_a, alpha_b

    pb_ref[...] = jnp.zeros_like(pb_ref)
    m0 = (jnp.full((1, rep * tq), MASKED_SCORE, F32),) * ATT_KV_HEADS
    one = (jnp.ones((1, rep * tq), F32),) * ATT_KV_HEADS
    _, _, alpha_b = lax.fori_loop(0, n_kc, att_body, (m0, scores_into(sta_ref, 0, 0), one))
    values_in(pb_ref, n_kc - 1, 1, alpha_b)
    for gi in range(ATT_KV_HEADS):
        acc = acc_ref[gi]
        og = (acc[0:ATT_HEAD_DIM] / acc[ATT_HEAD_DIM:ATT_HEAD_DIM + 1]).astype(BF16)
        for sb in range(n_qs):
            for r in range(rep):
                hd = gi * rep + r
                col = (sb * rep + r) * LANES
                og_ref[hd * ATT_HEAD_DIM:(hd + 1) * ATT_HEAD_DIM, sb * LANES:(sb + 1) * LANES] = og[:, col:col + LANES]

    o_ref[0] = h_ref[0] + _dot_tn(og_ref[...], wo_ref[...])


def _dsa_key_chunk(s):
    return min(DSA_KEY_CHUNK, s)


def _dsa(h, q, k, v_t, qi, ki, wi_t, w_out):
    b, s, d = h.shape
    tq = DSA_QUERY_TILE
    top_k = min(TOPK_MAX, s // 4)
    kc = _dsa_key_chunk(s)
    assert kc >= top_k and s % kc == 0 and kc % (2 * tq) == 0
    rep = ATT_HEADS // ATT_KV_HEADS
    nkv = ATT_KV_HEADS * ATT_HEAD_DIM
    n_chunks = s // kc
    vrows = ATT_HEAD_DIM + BF16_SUBLANES
    ka = kc // 2
    upper = jnp.asarray(np.arange(ka)[None, :] > np.arange(ka)[:, None], BF16)
    return pl.pallas_call(
        functools.partial(_dsa_kernel, tq=tq, kc=kc, top_k=top_k),
        grid=(b, s // tq),
        in_specs=[
            pl.BlockSpec((1, ATT_HEADS, tq, ATT_HEAD_DIM), lambda bi, qi_: (bi, 0, qi_, 0)),
            pl.BlockSpec((1, IDX_HEADS, tq, IDX_DIM), lambda bi, qi_: (bi, 0, qi_, 0)),
            pl.BlockSpec((1, 8, tq), lambda bi, qi_: (bi, 0, qi_)),
            pl.BlockSpec((1, s, nkv), lambda bi, qi_: (bi, 0, 0)),
            pl.BlockSpec((1, ATT_KV_HEADS, n_chunks, ATT_HEAD_DIM, kc), lambda bi, qi_: (bi, 0, 0, 0, 0)),
            pl.BlockSpec((1, s, IDX_DIM), lambda bi, qi_: (bi, 0, 0)),
            pl.BlockSpec((1, tq, d), lambda bi, qi_: (bi, qi_, 0)),
            _const_spec((ATT_HEADS * ATT_HEAD_DIM, d)),
            _const_spec((ka, ka)),
        ],
        out_specs=pl.BlockSpec((1, tq, d), lambda bi, qi_: (bi, qi_, 0)),
        out_shape=jax.ShapeDtypeStruct((b, s, d), F32),
        scratch_shapes=[
            pltpu.VMEM((n_chunks, kc, tq), F32),
            pltpu.VMEM((n_chunks, kc, tq), BF16),
            pltpu.VMEM((ATT_KV_HEADS, tq // LANES, rep * LANES, 2 * ATT_HEAD_DIM), BF16),
            pltpu.VMEM((ATT_KV_HEADS, vrows, rep * tq), F32),
            pltpu.VMEM((ATT_HEADS * ATT_HEAD_DIM, tq), BF16),
            pltpu.VMEM((ATT_KV_HEADS, kc // 2, rep * tq), F32),
            pltpu.VMEM((ATT_KV_HEADS, kc // 2, rep * tq), F32),
            pltpu.VMEM((ATT_KV_HEADS, kc // 2, rep * tq), BF16),
            pltpu.VMEM((ATT_KV_HEADS, kc // 2, rep * tq), BF16),
        ],
        compiler_params=pltpu.CompilerParams(
            dimension_semantics=("arbitrary", "arbitrary"), vmem_limit_bytes=VMEM_LIMIT),
        name="dsa",
    )(q, qi, wi_t, k, v_t, ki, h, w_out.astype(BF16), upper)


def kernel(x, positions, ffn1_norm, ffn1_wi, ffn1_wo, mix_norm, ffn2_norm, ffn2_wi, ffn2_wo, even_w_in, gla_gate_w, gla_gate_b, gla_out_norm, pool_w, pool_scale, even_w_out, odd_w_in, odd_w_out, final_norm):
    b, s, d = x.shape
    depth = ffn1_wi.shape[0]
    h = x
    ffn1_wi, ffn1_wo, ffn2_wi, ffn2_wo = (w.astype(BF16) for w in (ffn1_wi, ffn1_wo, ffn2_wi, ffn2_wo))
    for li in range(depth):
        h = _ffn(h.reshape(b * s, d), ffn1_norm[li], ffn1_wi, ffn1_wo, li, final_norm, final=False)
        h = h.reshape(b, s, d)
        j = li // 2
        if li % 2 == 0:
            h = _even_mixer(h, mix_norm[li], even_w_in[j], gla_gate_w[j], gla_gate_b[j], gla_out_norm[j],
                            pool_w[j], pool_scale[j], even_w_out[j])
        else:
            q, k, v_t, qi, ki, wi_t = _odd_proj(h, mix_norm[li], odd_w_in[j], positions)
            h = _dsa(h, q, k, v_t, qi, ki, wi_t, odd_w_out[j])
        last = li == depth - 1
        h = _ffn(h.reshape(b * s, d), ffn2_norm[li], ffn2_wi, ffn2_wo, li, final_norm, final=last)
        h = h.reshape(b, s, d)
    return h
```

```python
import functools
import math

import jax
import jax.numpy as jnp
import numpy as np
from jax import lax
from jax.experimental import pallas as pl
from jax.experimental.pallas import tpu as pltpu

F32 = jnp.float32
BF16 = jnp.bfloat16

RMS_EPS = 1e-6
ROPE_THETA = 10000.0

GLA_HEADS = 4
GLA_DK = 64
GLA_DV = 128
GLA_GATE_RANK = 16
GLA_GATE_TAU = 16.0

POOL_WINDOWS = (2, 4, 8, 16)
POOL_GROUP_DIM = 128
POOL_HALO = 16

ATT_HEADS = 8
ATT_KV_HEADS = 2
ATT_HEAD_DIM = 128
IDX_HEADS = 4
IDX_DIM = 64
TOPK_MAX = 256
DSA_QUERY_TILE = 256

LANES = 128
BF16_SUBLANES = 16
MXU_WIDTH = 256
MASKED_SCORE = -1e30
SCAN_ROWS = 128
COUNT_ROWS = 32
BISECT_STEPS = 16
DSA_KEY_CHUNK = 512
LOG2_E = 1.4426950408889634
VMEM_LIMIT = 56 * 1024 * 1024


def _rms(x, g):
    ms = jnp.mean(x * x, axis=-1, keepdims=True)
    return x * lax.rsqrt(ms + RMS_EPS) * g


def _dot(a, b):
    return jnp.dot(a, b, preferred_element_type=F32)


def _dot_nt(a, b):
    return lax.dot_general(a, b, (((1,), (1,)), ((), ())), preferred_element_type=F32)


def _dot_tn(a, b):
    return lax.dot_general(a, b, (((0,), (0,)), ((), ())), preferred_element_type=F32)


def _const_spec(shape):
    n = len(shape)
    return pl.BlockSpec(shape, lambda *_: (0,) * n)


def _ffn_kernel(h_ref, g_ref, wi_ref, wo_ref, fg_ref, o_ref, xn_ref, acc_ref, *, f, tf, final):
    x = h_ref[...]
    xn_ref[...] = _rms(x, g_ref[...]).astype(BF16)
    for c in range(f // tf):
        xn = xn_ref[...]
        gate = _dot(xn, wi_ref[0, :, c * tf:(c + 1) * tf].astype(BF16))
        up = _dot(xn, wi_ref[0, :, f + c * tf:f + (c + 1) * tf].astype(BF16))
        act = gate * (1.0 / (1.0 + jnp.exp(-gate))) * up
        y = _dot(act.astype(BF16), wo_ref[0, c * tf:(c + 1) * tf, :].astype(BF16))
        if c == 0:
            acc_ref[...] = y
        else:
            acc_ref[...] += y
    out = x + 0.5 * acc_ref[...]
    if final:
        out = _rms(out, fg_ref[...])
    o_ref[...] = out


def _ffn(h, norm_g, wi_all, wo_all, layer, final_g, *, final, tm=512, tf=MXU_WIDTH):
    n, d = h.shape
    f = wo_all.shape[1]
    assert f % tf == 0 and n % tm == 0
    return pl.pallas_call(
        functools.partial(_ffn_kernel, f=f, tf=tf, final=final),
        grid=(n // tm,),
        in_specs=[
            pl.BlockSpec((tm, d), lambda i: (i, 0)),
            _const_spec((1, d)),
            pl.BlockSpec((1, d, 2 * f), lambda i: (layer, 0, 0), pipeline_mode=pl.Buffered(1)),
            pl.BlockSpec((1, f, d), lambda i: (layer, 0, 0), pipeline_mode=pl.Buffered(1)),
            _const_spec((1, d)),
        ],
        out_specs=pl.BlockSpec((tm, d), lambda i: (i, 0)),
        out_shape=jax.ShapeDtypeStruct((n, d), F32),
        scratch_shapes=[pltpu.VMEM((tm, d), BF16), pltpu.VMEM((tm, d), F32)],
        compiler_params=pltpu.CompilerParams(
            dimension_semantics=("arbitrary",), vmem_limit_bytes=VMEM_LIMIT),
        name="ffn_final" if final else "ffn",
    )(h, norm_g.reshape(1, d), wi_all, wo_all, final_g.reshape(1, d))


def _log_sigmoid(z):
    return jnp.minimum(z, 0.0) - jnp.log1p(jnp.exp(-jnp.abs(z)))


def _split_bf16(x, n):
    pieces = []
    for _ in range(n - 1):
        hi = x.astype(BF16)
        pieces.append(hi)
        x = x - hi.astype(F32)
    pieces.append(x.astype(BF16))
    return pieces


def _dot_pieces(sel, x, n):
    pieces = _split_bf16(x, n)
    out = _dot(sel, pieces[0])
    for piece in pieces[1:]:
        out = out + _dot(sel, piece)
    return out


def _gla_levels(tt):
    return [tt >> i for i in range(tt.bit_length() - 1)]


def _gla_constants(tt):
    levels = _gla_levels(tt)
    row = np.arange(tt)[:, None]
    col = np.arange(tt)[None, :]
    tril = (col <= row).astype(np.float32)
    gsel = np.concatenate([(col == (row // s) * s + s // 2 - 1).astype(np.float32) for s in levels], axis=0)
    lmask = np.stack([((row // s == col // s) & (row % s >= s // 2) & (col % s < s // 2)).astype(np.float32)
                      for s in levels])
    return jnp.asarray(tril, BF16), jnp.asarray(gsel, BF16), jnp.asarray(lmask, F32)


def _even_kernel(h_ref, g_ref, wm_ref, wa_ref, gw_ref, gb_ref, on_ref, pw_ref, ps_ref, wo_ref,
                 tril_ref, gsel_ref, lmask_ref, o_ref, state_ref, ext_ref, mix_ref, *, tt):
    t_idx = pl.program_id(1)
    hk = GLA_HEADS * GLA_DK
    hv = GLA_HEADS * GLA_DV

    @pl.when(t_idx == 0)
    def _():
        state_ref[...] = jnp.zeros_like(state_ref)
        ext_ref[0:POOL_HALO, :] = jnp.zeros((POOL_HALO, ext_ref.shape[1]), F32)

    x = h_ref[0]
    hn = _rms(x, g_ref[...]).astype(BF16)
    a_lr = _dot(hn, wa_ref[...])
    a_hi, a_lo = _split_bf16(a_lr, 2)
    z = _dot(a_hi, gw_ref[0]) + (_dot(a_hi, gw_ref[1]) + _dot(a_lo, gw_ref[0])) + gb_ref[...]
    la = _log_sigmoid(z) * (1.0 / GLA_GATE_TAU)
    proj_u = _dot(hn, wm_ref[:, 2 * hk + 2 * hv:])

    bc = _dot_pieces(tril_ref[...], la, 3)
    proj_v = _dot(hn, wm_ref[:, 2 * hk:2 * hk + hv])

    levels = _gla_levels(tt)
    ref_all = _dot_pieces(gsel_ref[...], bc, 2)

    ext_ref[POOL_HALO:POOL_HALO + tt, :] = proj_u
    tpos = (t_idx * tt + lax.broadcasted_iota(jnp.int32, (tt, 1), 0) + 1).astype(F32)
    for gi, w in enumerate(POOL_WINDOWS):
        ls = slice(gi * POOL_GROUP_DIM, (gi + 1) * POOL_GROUP_DIM)
        ug = ext_ref[POOL_HALO:POOL_HALO + tt, ls]
        win = ug
        for dlt in range(1, w):
            win = win + ext_ref[POOL_HALO - dlt:POOL_HALO - dlt + tt, ls]
        pg = win / jnp.minimum(tpos, float(w)) - ug
        pg = _dot(pg.astype(BF16), pw_ref[gi]) * ps_ref[:, ls]
        mix_ref[:, hv + gi * POOL_GROUP_DIM: hv + (gi + 1) * POOL_GROUP_DIM] = pg.astype(BF16)
    ext_ref[0:POOL_HALO, :] = ext_ref[tt:tt + POOL_HALO, :]

    proj_qk = _dot(hn, wm_ref[:, 0:2 * hk])
    proj_g = _dot(hn, wm_ref[:, 2 * hk + hv:2 * hk + 2 * hv])
    rowk = lax.broadcasted_iota(jnp.int32, (tt, hk), 0)
    q_scaled = []
    k_scaled = []
    for li, s in enumerate(levels):
        ref_b = ref_all[li * tt:(li + 1) * tt]
        upper = (rowk & (s - 1)) >= s // 2
        q_scaled.append(jnp.exp(jnp.where(upper, bc - ref_b, 0.0)))
        k_scaled.append(jnp.exp(jnp.where(upper, 0.0, ref_b - bc)))

    e_b = jnp.exp(bc)
    b_last = bc[tt - 1:tt, :]
    e_last = jnp.exp(b_last)
    e_dec = jnp.exp(b_last - bc)

    for hd in range(GLA_HEADS):
        ks = slice(hd * GLA_DK, (hd + 1) * GLA_DK)
        qh = proj_qk[:, ks] * (GLA_DK ** -0.5)
        kh = proj_qk[:, hk + hd * GLA_DK: hk + (hd + 1) * GLA_DK]
        vh = proj_v[:, hd * GLA_DV:(hd + 1) * GLA_DV]
        vh_b = vh.astype(BF16)
        scores = jnp.zeros((tt, tt), F32)
        for li in range(len(levels)):
            qs = (qh * q_scaled[li][:, ks]).astype(BF16)
            kss = (kh * k_scaled[li][:, ks]).astype(BF16)
            scores = scores + jnp.where(lmask_ref[li] > 0.0, _dot_nt(qs, kss), 0.0)
        diag = jnp.sum(qh * kh, axis=-1, keepdims=True)
        st = state_ref[hd]
        inter = _dot_nt((qh * e_b[:, ks]).astype(BF16), st.astype(BF16))
        o_h = inter + _dot(scores.astype(BF16), vh_b) + diag * vh
        state_ref[hd] = st * e_last[:, ks] + _dot_tn(vh_b, (kh * e_dec[:, ks]).astype(BF16))
        gh = proj_g[:, hd * GLA_DV:(hd + 1) * GLA_DV]
        o_n = _rms(o_h, on_ref[:, hd * GLA_DV:(hd + 1) * GLA_DV])
        mix_ref[:, hd * GLA_DV:(hd + 1) * GLA_DV] = (o_n * gh * (1.0 / (1.0 + jnp.exp(-gh)))).astype(BF16)

    o_ref[0] = x + _dot(mix_ref[...], wo_ref[...])


def _even_mixer(h, norm_g, w_in, gate_w, gate_b, out_norm, pool_w, pool_scale, w_out, *, tt=256):
    b, s, d = h.shape
    hk = GLA_HEADS * GLA_DK
    hv = GLA_HEADS * GLA_DV
    pool_dim = len(POOL_WINDOWS) * POOL_GROUP_DIM
    o_a = 2 * hk + 2 * hv
    w_main = jnp.concatenate([w_in[:, :o_a], w_in[:, o_a + GLA_GATE_RANK:]], axis=1).astype(BF16)
    w_a = jnp.pad(w_in[:, o_a:o_a + GLA_GATE_RANK], ((0, 0), (0, LANES - GLA_GATE_RANK))).astype(BF16)
    gw = jnp.pad(gate_w, ((0, LANES - GLA_GATE_RANK), (0, 0)))
    gw_hi = gw.astype(BF16)
    gw = jnp.stack([gw_hi, (gw - gw_hi.astype(F32)).astype(BF16)])
    tril, gsel, lmask = _gla_constants(tt)
    n_levels = lmask.shape[0]
    n_main = w_main.shape[1]
    mix_dim = hv + pool_dim
    return pl.pallas_call(
        functools.partial(_even_kernel, tt=tt),
        grid=(b, s // tt),
        in_specs=[
            pl.BlockSpec((1, tt, d), lambda bi, ti: (bi, ti, 0)),
            _const_spec((1, d)),
            _const_spec((d, n_main)),
            _const_spec((d, LANES)),
            _const_spec((2, LANES, hk)),
            _const_spec((1, hk)),
            _const_spec((1, hv)),
            _const_spec((len(POOL_WINDOWS), POOL_GROUP_DIM, POOL_GROUP_DIM)),
            _const_spec((1, pool_dim)),
            _const_spec((mix_dim, d)),
            _const_spec((tt, tt)),
            _const_spec((n_levels * tt, tt)),
            _const_spec((n_levels, tt, tt)),
        ],
        out_specs=pl.BlockSpec((1, tt, d), lambda bi, ti: (bi, ti, 0)),
        out_shape=jax.ShapeDtypeStruct((b, s, d), F32),
        scratch_shapes=[
            pltpu.VMEM((GLA_HEADS, GLA_DV, GLA_DK), F32),
            pltpu.VMEM((POOL_HALO + tt, pool_dim), F32),
            pltpu.VMEM((tt, mix_dim), BF16),
        ],
        compiler_params=pltpu.CompilerParams(
            dimension_semantics=("arbitrary", "arbitrary"), vmem_limit_bytes=VMEM_LIMIT),
        name="even_mixer",
    )(h, norm_g.reshape(1, d), w_main, w_a, gw, gate_b.reshape(1, hk), out_norm.reshape(1, hv),
      pool_w.astype(BF16), pool_scale.reshape(1, pool_dim), w_out.astype(BF16), tril, gsel, lmask)


def _odd_proj_kernel(h_ref, g_ref, w_ref, pos_ref, inv_ref, q_ref, k_ref, vt_ref, qi_ref, ki_ref, wi_ref, *, tm):
    nq = ATT_HEADS * ATT_HEAD_DIM
    nkv = ATT_KV_HEADS * ATT_HEAD_DIM
    ni = IDX_HEADS * IDX_DIM
    hn = _rms(h_ref[0], g_ref[...]).astype(BF16)

    def proj_cols(lo, width):
        return _dot(hn, w_ref[:, lo:lo + width])

    per_dot = MXU_WIDTH // ATT_HEAD_DIM
    o_i = nq + 2 * nkv
    xv = proj_cols(nq + nkv, nkv)
    for gi in range(ATT_KV_HEADS):
        vt_ref[0, gi, 0] = xv[:, gi * ATT_HEAD_DIM:(gi + 1) * ATT_HEAD_DIM].T.astype(BF16)
    xqs = [proj_cols(h0 * ATT_HEAD_DIM, MXU_WIDTH) for h0 in range(0, ATT_HEADS, per_dot)]
    xk = proj_cols(nq, nkv)
    xi = proj_cols(o_i, ni)
    tail = proj_cols(o_i + ni, LANES)
    wi_ref[0] = pltpu.roll(tail, LANES - IDX_DIM, 1).T[0:8, :]

    pos = pos_ref[0]
    lane = lax.broadcasted_iota(jnp.int32, (tm, LANES), 1)

    half_a = ATT_HEAD_DIM // 2
    half_i = IDX_DIM // 2
    ang = pos * inv_ref[0:1, :]
    cos_p = jnp.cos(ang)
    sin_p = jnp.sin(ang)

    def att_table(t):
        return jnp.where(lane < half_a, t, pltpu.roll(t, half_a, 1))

    def idx_table(t):
        t32 = jnp.where((lane & (2 * half_i - 1)) < half_i, pltpu.roll(t, half_a, 1), pltpu.roll(t, half_a + half_i, 1))
        return jnp.where(lane < IDX_DIM, t32, pltpu.roll(t32, IDX_DIM, 1))

    cos_a = att_table(cos_p)
    sin_a = att_table(sin_p)
    sin_a = jnp.where(lane < half_a, -sin_a, sin_a)

    def rope_att(xh):
        return xh * cos_a + pltpu.roll(xh, ATT_HEAD_DIM // 2, 1) * sin_a

    cos_i = idx_table(cos_p)
    sin_i = idx_table(sin_p)
    low_half = (lane % IDX_DIM) < IDX_DIM // 2

    def rope_idx(xb):
        rot = jnp.where(low_half, -pltpu.roll(xb, LANES - IDX_DIM // 2, 1), pltpu.roll(xb, IDX_DIM // 2, 1))
        return xb * cos_i + rot * sin_i

    for g, xq in enumerate(xqs):
        for hd in range(per_dot):
            xh = xq[:, hd * ATT_HEAD_DIM:(hd + 1) * ATT_HEAD_DIM]
            q_ref[0, g * per_dot + hd] = (rope_att(xh) * (LOG2_E * ATT_HEAD_DIM ** -0.5)).astype(BF16)
    for gi in range(ATT_KV_HEADS):
        xh = xk[:, gi * ATT_HEAD_DIM:(gi + 1) * ATT_HEAD_DIM]
        k_ref[0, :, gi * ATT_HEAD_DIM:(gi + 1) * ATT_HEAD_DIM] = rope_att(xh).astype(BF16)
    for pi in range(ni // LANES):
        xr = rope_idx(xi[:, pi * LANES:(pi + 1) * LANES]).astype(BF16)
        for sub in range(LANES // IDX_DIM):
            qi_ref[0, pi * (LANES // IDX_DIM) + sub] = xr[:, sub * IDX_DIM:(sub + 1) * IDX_DIM]
    ki_ref[0] = rope_idx(tail)[:, :IDX_DIM].astype(BF16)


def _odd_proj(h, norm_g, w_in, positions):
    b, s, d = h.shape
    tm = _dsa_key_chunk(s)
    n_in = w_in.shape[1]
    n_pad = -(-n_in // LANES) * LANES
    w = jnp.pad(w_in, ((0, 0), (0, n_pad - n_in))).astype(BF16)
    inv_a = ROPE_THETA ** (-jnp.arange(0, ATT_HEAD_DIM, 2, dtype=F32) / ATT_HEAD_DIM)
    inv_i = ROPE_THETA ** (-jnp.arange(0, IDX_DIM, 2, dtype=F32) / IDX_DIM)
    inv = jnp.concatenate([inv_a, inv_i])
    inv = jnp.pad(inv[None, :], ((0, 7), (0, LANES - inv.shape[0])))
    pos = positions.astype(F32).reshape(b, s, 1)
    return pl.pallas_call(
        functools.partial(_odd_proj_kernel, tm=tm),
        grid=(b, s // tm),
        in_specs=[
            pl.BlockSpec((1, tm, d), lambda bi, ti: (bi, ti, 0)),
            _const_spec((1, d)),
            _const_spec((d, n_pad)),
            pl.BlockSpec((1, tm, 1), lambda bi, ti: (bi, ti, 0)),
            _const_spec((8, LANES)),
        ],
        out_specs=[
            pl.BlockSpec((1, ATT_HEADS, tm, ATT_HEAD_DIM), lambda bi, ti: (bi, 0, ti, 0)),
            pl.BlockSpec((1, tm, ATT_KV_HEADS * ATT_HEAD_DIM), lambda bi, ti: (bi, ti, 0)),
            pl.BlockSpec((1, ATT_KV_HEADS, 1, ATT_HEAD_DIM, tm), lambda bi, ti: (bi, 0, ti, 0, 0)),
            pl.BlockSpec((1, IDX_HEADS, tm, IDX_DIM), lambda bi, ti: (bi, 0, ti, 0)),
            pl.BlockSpec((1, tm, IDX_DIM), lambda bi, ti: (bi, ti, 0)),
            pl.BlockSpec((1, 8, tm), lambda bi, ti: (bi, 0, ti)),
        ],
        out_shape=[
            jax.ShapeDtypeStruct((b, ATT_HEADS, s, ATT_HEAD_DIM), BF16),
            jax.ShapeDtypeStruct((b, s, ATT_KV_HEADS * ATT_HEAD_DIM), BF16),
            jax.ShapeDtypeStruct((b, ATT_KV_HEADS, s // tm, ATT_HEAD_DIM, tm), BF16),
            jax.ShapeDtypeStruct((b, IDX_HEADS, s, IDX_DIM), BF16),
            jax.ShapeDtypeStruct((b, s, IDX_DIM), BF16),
            jax.ShapeDtypeStruct((b, 8, s), F32),
        ],
        compiler_params=pltpu.CompilerParams(
            dimension_semantics=("arbitrary", "arbitrary"), vmem_limit_bytes=VMEM_LIMIT),
        name="odd_proj",
    )(h, norm_g.reshape(1, d), w, pos, inv)


def _float_to_ordered(x):
    bits = lax.bitcast_convert_type(x, jnp.int32)
    return jnp.where(bits >= 0, bits, bits ^ jnp.int32(0x7FFFFFFF))


def _ordered_to_float(o):
    bits = jnp.where(o >= 0, o, o ^ jnp.int32(0x7FFFFFFF))
    return lax.bitcast_convert_type(bits, F32)


def _dsa_kernel(q_ref, qi_ref, wi_ref, k_ref, vt_ref, ki_ref, h_ref, wo_ref, upper_ref, o_ref,
                sc_ref, bias_ref, qe_ref, acc_ref, og_ref, sta_ref, stb_ref, pa_ref, pb_ref, *, tq, kc, top_k):
    qb = pl.program_id(1)
    n_kc = (qb * tq + tq + kc - 1) // kc
    rep = ATT_HEADS // ATT_KV_HEADS
    q_pos = qb * tq + lax.broadcasted_iota(jnp.int32, (1, tq), 1)
    scan_iota = lax.broadcasted_iota(jnp.int32, (SCAN_ROWS, tq), 0)

    qi = qi_ref[0].reshape(IDX_HEADS * tq, IDX_DIM)
    wi = wi_ref[0] * ((IDX_DIM ** -0.5) * (IDX_HEADS ** -0.5))

    def fold(vals, op):
        return op(vals.reshape(vals.shape[0] // COUNT_ROWS, COUNT_ROWS, tq), axis=0)

    def score_body(c, carry):
        mx_p, mn_p = carry
        for r0 in range(0, kc, SCAN_ROWS):
            off = pl.multiple_of(c * kc + r0, SCAN_ROWS)
            lg = jnp.maximum(_dot_nt(ki_ref[0, pl.ds(off, SCAN_ROWS), :], qi), 0.0)
            isc = lg[:, 0:tq] * wi[0:1, :]
            for hd in range(1, IDX_HEADS):
                isc = isc + lg[:, hd * tq:(hd + 1) * tq] * wi[hd:hd + 1, :]
            sc = jnp.where(scan_iota + off <= q_pos, isc, -jnp.inf)
            sc_ref[c, r0:r0 + SCAN_ROWS, :] = sc
            mx_p = jnp.maximum(mx_p, fold(sc, jnp.max))
            mn_p = jnp.minimum(mn_p, fold(isc, jnp.min))
        return mx_p, mn_p

    mx_p, mn_p = lax.fori_loop(0, n_kc, score_body, (jnp.full((COUNT_ROWS, tq), -jnp.inf, F32),
                                                     jnp.full((COUNT_ROWS, tq), jnp.inf, F32)))
    mx = jnp.max(mx_p, axis=0, keepdims=True) + 0.0
    mn = jnp.min(mn_p, axis=0, keepdims=True)

    def count_ge(t):
        def body(c, acc):
            for r0 in range(0, kc, SCAN_ROWS):
                xs = sc_ref[c, r0:r0 + SCAN_ROWS, :].reshape(SCAN_ROWS // COUNT_ROWS, COUNT_ROWS, tq)
                acc = acc + jnp.sum(jnp.where(xs >= t, 1.0, 0.0), axis=0)
            return acc
        part = lax.fori_loop(0, n_kc, body, jnp.zeros((COUNT_ROWS, tq), F32))
        return jnp.sum(part, axis=0, keepdims=True)

    def max_below(t):
        def body(c, acc):
            for r0 in range(0, kc, SCAN_ROWS):
                xs = sc_ref[c, r0:r0 + SCAN_ROWS, :]
                acc = jnp.maximum(acc, fold(jnp.where(xs < t, xs, -jnp.inf), jnp.max))
            return acc
        part = lax.fori_loop(0, n_kc, body, jnp.full((COUNT_ROWS, tq), -jnp.inf, F32))
        return jnp.max(part, axis=0, keepdims=True)

    kf = float(top_k)
    n_adm = (q_pos + 1).astype(F32)
    small = n_adm <= kf
    state = (jnp.where(small, 0.0, mn),
             jnp.where(small, 1.0, _ordered_to_float(_float_to_ordered(mx) + 1)),
             n_adm,
             jnp.where(small, 1.0, 0.0),
             jnp.full((1, tq), -jnp.inf, F32),
             jnp.full((1, tq), kf, F32))

    def bis_body(_, st):
        lo, hi, c_lo, done, thr, c_thr = st
        mid = 0.5 * lo + 0.5 * hi
        c = count_ge(mid)
        active = done == 0.0
        stuck = (mid <= lo) | (mid >= hi)
        fin_stuck = active & stuck
        fin_hit = active & (~stuck) & (c == kf)
        up = active & (~stuck) & (c >= kf)
        down = active & (~stuck) & (c < kf)
        thr = jnp.where(fin_stuck, lo, jnp.where(fin_hit, mid, thr))
        c_thr = jnp.where(fin_stuck, c_lo, c_thr)
        done = jnp.where(fin_stuck | fin_hit, 1.0, done)
        return (jnp.where(up, mid, lo), jnp.where(down, mid, hi), jnp.where(up, c, c_lo), done, thr, c_thr)

    _, hi, _, done, thr, c_thr = lax.fori_loop(0, BISECT_STEPS, bis_body, state)

    def peel_body(st):
        _, hi, done, thr, c_thr = st
        v = max_below(hi)
        c = count_ge(v)
        active = done == 0.0
        ok = active & (c >= kf)
        done = jnp.where(ok, 1.0, done)
        return (jnp.sum(1.0 - done), jnp.where(active & (c < kf), v, hi), done,
                jnp.where(ok, v, thr), jnp.where(ok, c, c_thr))

    _, _, _, thr, c_thr = lax.while_loop(lambda st: st[0] > 0.0, peel_body,
                                         (jnp.sum(1.0 - done), hi, done, thr, c_thr))

    ka = kc // 2
    excess = c_thr - kf
    half_iota = lax.broadcasted_iota(jnp.int32, (ka, tq), 0)

    def mask_body(j, later_total):
        c = n_kc - 1 - j
        halves = []
        for half in (1, 0):
            xs = sc_ref[c, half * ka:(half + 1) * ka, :]
            ties = jnp.where(xs == thr, 1.0, 0.0)
            halves.append((half, xs, ties, _dot(upper_ref[...], ties.astype(BF16))))
        for half, xs, ties, later_in_half in halves:
            later = later_in_half + later_total
            sel = jnp.where(xs > thr, 0.0,
                            jnp.where(ties > 0.0, jnp.where(later >= excess, 0.0, MASKED_SCORE), MASKED_SCORE))
            kidx = half_iota + (c * kc + half * ka)
            bias_ref[c, half * ka:(half + 1) * ka, :] = jnp.where(kidx <= q_pos, sel, MASKED_SCORE).astype(BF16)
            later_total = later[0:1, :] + ties[0:1, :]
        return later_total

    lax.fori_loop(0, n_kc, mask_body, jnp.zeros((1, tq), F32))

    acc_ref[...] = jnp.zeros_like(acc_ref)
    ones_rows = jnp.ones((BF16_SUBLANES, ka), BF16)
    n_qs = tq // LANES
    own_col = ((lax.broadcasted_iota(jnp.int32, (rep * LANES, LANES), 0) & (LANES - 1))
               == lax.broadcasted_iota(jnp.int32, (rep * LANES, LANES), 1))
    for gi in range(ATT_KV_HEADS):
        for sb in range(n_qs):
            qs = q_ref[0, gi * rep:(gi + 1) * rep, sb * LANES:(sb + 1) * LANES, :]
            qe_ref[gi, sb, :, 0:ATT_HEAD_DIM] = qs.reshape(rep * LANES, ATT_HEAD_DIM)
            qe_ref[gi, sb, :, ATT_HEAD_DIM:] = jnp.where(own_col, 1.0, 0.0).astype(BF16)

    def scores_into(dst_ref, c, half):
        off = pl.multiple_of(c * kc + half * ka, ka)
        bias = bias_ref[c, half * ka:(half + 1) * ka, :]
        tops = []
        for gi in range(ATT_KV_HEADS):
            kb = k_ref[0, pl.ds(off, ka), gi * ATT_HEAD_DIM:(gi + 1) * ATT_HEAD_DIM]
            st = jnp.concatenate(
                [_dot_nt(jnp.concatenate([kb, bias[:, sb * LANES:(sb + 1) * LANES]], axis=1), qe_ref[gi, sb])
                 for sb in range(n_qs)], axis=1)
            dst_ref[gi] = st
            tops.append(jnp.max(st, axis=0, keepdims=True))
        return tuple(tops)

    def softmax_into(p_ref, src_ref, m_all, tops):
        m_out, alphas = [], []
        for gi in range(ATT_KV_HEADS):
            m_new = jnp.maximum(m_all[gi], tops[gi])
            alphas.append(jnp.exp2(m_all[gi] - m_new))
            p_ref[gi] = jnp.exp2(src_ref[gi] - m_new).astype(BF16)
            m_out.append(m_new)
        return tuple(m_out), tuple(alphas)

    def values_in(p_ref, c, half, alphas):
        for gi in range(ATT_KV_HEADS):
            vt = jnp.concatenate([vt_ref[0, gi, c, :, half * ka:(half + 1) * ka], ones_rows], axis=0)
            acc_ref[gi] = alphas[gi] * acc_ref[gi] + _dot(vt, p_ref[gi])

    def att_body(c, carry):
        m_all, tops_a, alpha_b = carry
        tops_b = scores_into(stb_ref, c, 1)
        values_in(pb_ref, jnp.maximum(c - 1, 0), 1, alpha_b)
        m_all, alpha_a = softmax_into(pa_ref, sta_ref, m_all, tops_a)
        tops_a = scores_into(sta_ref, jnp.minimum(c + 1, n_kc - 1), 0)
        values_in(pa_ref, c, 0, alpha_a)
        m_all, alpha_b = softmax_into(pb_ref, stb_ref, m_all, tops_b)
        return m_all, tops_a, alpha_b

    pb_ref[...] = jnp.zeros_like(pb_ref)
    m0 = (jnp.full((1, rep * tq), MASKED_SCORE, F32),) * ATT_KV_HEADS
    one = (jnp.ones((1, rep * tq), F32),) * ATT_KV_HEADS
    _, _, alpha_b = lax.fori_loop(0, n_kc, att_body, (m0, scores_into(sta_ref, 0, 0), one))
    values_in(pb_ref, n_kc - 1, 1, alpha_b)
    for gi in range(ATT_KV_HEADS):
        acc = acc_ref[gi]
        og = (acc[0:ATT_HEAD_DIM] / acc[ATT_HEAD_DIM:ATT_HEAD_DIM + 1]).astype(BF16)
        for sb in range(n_qs):
            for r in range(rep):
                hd = gi * rep + r
                col = (sb * rep + r) * LANES
                og_ref[hd * ATT_HEAD_DIM:(hd + 1) * ATT_HEAD_DIM, sb * LANES:(sb + 1) * LANES] = og[:, col:col + LANES]

    o_ref[0] = h_ref[0] + _dot_tn(og_ref[...], wo_ref[...])


def _dsa_key_chunk(s):
    return min(DSA_KEY_CHUNK, s)


def _dsa(h, q, k, v_t, qi, ki, wi_t, w_out):
    b, s, d = h.shape
    tq = DSA_QUERY_TILE
    top_k = min(TOPK_MAX, s // 4)
    kc = _dsa_key_chunk(s)
    assert kc >= top_k and s % kc == 0 and kc % (2 * tq) == 0
    rep = ATT_HEADS // ATT_KV_HEADS
    nkv = ATT_KV_HEADS * ATT_HEAD_DIM
    n_chunks = s // kc
    vrows = ATT_HEAD_DIM + BF16_SUBLANES
    ka = kc // 2
    upper = jnp.asarray(np.arange(ka)[None, :] > np.arange(ka)[:, None], BF16)
    return pl.pallas_call(
        functools.partial(_dsa_kernel, tq=tq, kc=kc, top_k=top_k),
        grid=(b, s // tq),
        in_specs=[
            pl.BlockSpec((1, ATT_HEADS, tq, ATT_HEAD_DIM), lambda bi, qi_: (bi, 0, qi_, 0)),
            pl.BlockSpec((1, IDX_HEADS, tq, IDX_DIM), lambda bi, qi_: (bi, 0, qi_, 0)),
            pl.BlockSpec((1, 8, tq), lambda bi, qi_: (bi, 0, qi_)),
            pl.BlockSpec((1, s, nkv), lambda bi, qi_: (bi, 0, 0)),
            pl.BlockSpec((1, ATT_KV_HEADS, n_chunks, ATT_HEAD_DIM, kc), lambda bi, qi_: (bi, 0, 0, 0, 0)),
            pl.BlockSpec((1, s, IDX_DIM), lambda bi, qi_: (bi, 0, 0)),
            pl.BlockSpec((1, tq, d), lambda bi, qi_: (bi, qi_, 0)),
            _const_spec((ATT_HEADS * ATT_HEAD_DIM, d)),
            _const_spec((ka, ka)),
        ],
        out_specs=pl.BlockSpec((1, tq, d), lambda bi, qi_: (bi, qi_, 0)),
        out_shape=jax.ShapeDtypeStruct((b, s, d), F32),
        scratch_shapes=[
            pltpu.VMEM((n_chunks, kc, tq), F32),
            pltpu.VMEM((n_chunks, kc, tq), BF16),
            pltpu.VMEM((ATT_KV_HEADS, tq // LANES, rep * LANES, 2 * ATT_HEAD_DIM), BF16),
            pltpu.VMEM((ATT_KV_HEADS, vrows, rep * tq), F32),
            pltpu.VMEM((ATT_HEADS * ATT_HEAD_DIM, tq), BF16),
            pltpu.VMEM((ATT_KV_HEADS, kc // 2, rep * tq), F32),
            pltpu.VMEM((ATT_KV_HEADS, kc // 2, rep * tq), F32),
            pltpu.VMEM((ATT_KV_HEADS, kc // 2, rep * tq), BF16),
            pltpu.VMEM((ATT_KV_HEADS, kc // 2, rep * tq), BF16),
        ],
        compiler_params=pltpu.CompilerParams(
            dimension_semantics=("arbitrary", "arbitrary"), vmem_limit_bytes=VMEM_LIMIT),
        name="dsa",
    )(q, qi, wi_t, k, v_t, ki, h, w_out.astype(BF16), upper)


def kernel(x, positions, ffn1_norm, ffn1_wi, ffn1_wo, mix_norm, ffn2_norm, ffn2_wi, ffn2_wo, even_w_in, gla_gate_w, gla_gate_b, gla_out_norm, pool_w, pool_scale, even_w_out, odd_w_in, odd_w_out, final_norm):
    b, s, d = x.shape
    depth = ffn1_wi.shape[0]
    h = x
    for li in range(depth):
        h = _ffn(h.reshape(b * s, d), ffn1_norm[li], ffn1_wi, ffn1_wo, li, final_norm, final=False)
        h = h.reshape(b, s, d)
        j = li // 2
        if li % 2 == 0:
            h = _even_mixer(h, mix_norm[li], even_w_in[j], gla_gate_w[j], gla_gate_b[j], gla_out_norm[j],
                            pool_w[j], pool_scale[j], even_w_out[j])
        else:
            q, k, v_t, qi, ki, wi_t = _odd_proj(h, mix_norm[li], odd_w_in[j], positions)
            h = _dsa(h, q, k, v_t, qi, ki, wi_t, odd_w_out[j])
        last = li == depth - 1
        h = _ffn(h.reshape(b * s, d), ffn2_norm[li], ffn2_wi, ffn2_wo, li, final_norm, final=last)
        h = h.reshape(b, s, d)
    return h
```

```python
import functools

import jax
import jax.numpy as jnp
import numpy as np
from jax import lax
from jax.experimental import pallas as pl
from jax.experimental.pallas import tpu as pltpu

F32 = jnp.float32
BF16 = jnp.bfloat16

RMS_EPS = 1e-6
ROPE_THETA = 10000.0

GLA_HEADS = 4
GLA_DK = 64
GLA_DV = 128
GLA_GATE_RANK = 16
GLA_GATE_TAU = 16.0

POOL_WINDOWS = (2, 4, 8, 16)
POOL_GROUP_DIM = 128
POOL_HALO = 16

ATT_HEADS = 8
ATT_KV_HEADS = 2
ATT_HEAD_DIM = 128
IDX_HEADS = 4
IDX_DIM = 64
TOPK_MAX = 256
DSA_QUERY_TILE = 256

LANES = 128
SUBLANES = 8
BF16_SUBLANES = 16
FFN_TOKEN_TILE = 512
EVEN_TOKEN_TILE = 256
MXU_WIDTH = 256
MASKED_SCORE = -1e30
SCAN_ROWS = 128
COUNT_ROWS = 32
BISECT_STEPS = 16
DSA_KEY_CHUNK = 512
LOG2_E = 1.4426950408889634
VMEM_LIMIT = 56 * 1024 * 1024


def _rms(x, g):
    ms = jnp.mean(x * x, axis=-1, keepdims=True)
    return x * lax.rsqrt(ms + RMS_EPS) * g


def _dot(a, b):
    return jnp.dot(a, b, preferred_element_type=F32)


def _dot_nt(a, b):
    return lax.dot_general(a, b, (((1,), (1,)), ((), ())), preferred_element_type=F32)


def _dot_tn(a, b):
    return lax.dot_general(a, b, (((0,), (0,)), ((), ())), preferred_element_type=F32)


def _const_spec(shape):
    n = len(shape)
    return pl.BlockSpec(shape, lambda *_: (0,) * n)


def _ffn_kernel(h_ref, g_ref, wi_ref, wo_ref, fg_ref, o_ref, xn_ref, acc_ref, *, f, tf, final):
    x = h_ref[...]
    xn_ref[...] = _rms(x, g_ref[...]).astype(BF16)
    for c in range(f // tf):
        xn = xn_ref[...]
        gate = _dot(xn, wi_ref[0, :, c * tf:(c + 1) * tf].astype(BF16))
        up = _dot(xn, wi_ref[0, :, f + c * tf:f + (c + 1) * tf].astype(BF16))
        act = gate * (1.0 / (1.0 + jnp.exp(-gate))) * up
        y = _dot(act.astype(BF16), wo_ref[0, c * tf:(c + 1) * tf, :].astype(BF16))
        if c == 0:
            acc_ref[...] = y
        else:
            acc_ref[...] += y
    out = x + 0.5 * acc_ref[...]
    if final:
        out = _rms(out, fg_ref[...])
    o_ref[...] = out


def _ffn(h, norm_g, wi_all, wo_all, layer, final_g, *, final, tm=FFN_TOKEN_TILE, tf=MXU_WIDTH):
    n, d = h.shape
    f = wo_all.shape[1]
    assert f % tf == 0 and n % tm == 0
    return pl.pallas_call(
        functools.partial(_ffn_kernel, f=f, tf=tf, final=final),
        grid=(n // tm,),
        in_specs=[
            pl.BlockSpec((tm, d), lambda i: (i, 0)),
            _const_spec((1, d)),
            pl.BlockSpec((1, d, 2 * f), lambda i: (layer, 0, 0), pipeline_mode=pl.Buffered(1)),
            pl.BlockSpec((1, f, d), lambda i: (layer, 0, 0), pipeline_mode=pl.Buffered(1)),
            _const_spec((1, d)),
        ],
        out_specs=pl.BlockSpec((tm, d), lambda i: (i, 0)),
        out_shape=jax.ShapeDtypeStruct((n, d), F32),
        scratch_shapes=[pltpu.VMEM((tm, d), BF16), pltpu.VMEM((tm, d), F32)],
        compiler_params=pltpu.CompilerParams(
            dimension_semantics=("arbitrary",), vmem_limit_bytes=VMEM_LIMIT),
        name="ffn_final" if final else "ffn",
    )(h, norm_g.reshape(1, d), wi_all, wo_all, final_g.reshape(1, d))


def _log_sigmoid(z):
    return jnp.minimum(z, 0.0) - jnp.log1p(jnp.exp(-jnp.abs(z)))


def _split_bf16(x, n):
    pieces = []
    for _ in range(n - 1):
        hi = x.astype(BF16)
        pieces.append(hi)
        x = x - hi.astype(F32)
    pieces.append(x.astype(BF16))
    return pieces


def _dot_pieces(sel, x, n):
    pieces = _split_bf16(x, n)
    out = _dot(sel, pieces[0])
    for piece in pieces[1:]:
        out = out + _dot(sel, piece)
    return out


def _gla_levels(tt):
    return [tt >> i for i in range(tt.bit_length() - 1)]


def _gla_constants(tt):
    levels = _gla_levels(tt)
    row = np.arange(tt)[:, None]
    col = np.arange(tt)[None, :]
    tril = (col <= row).astype(np.float32)
    gsel = np.concatenate([(col == (row // s) * s + s // 2 - 1).astype(np.float32) for s in levels], axis=0)
    lmask = np.stack([((row // s == col // s) & (row % s >= s // 2) & (col % s < s // 2)).astype(np.float32)
                      for s in levels])
    return jnp.asarray(tril, BF16), jnp.asarray(gsel, BF16), jnp.asarray(lmask, F32)


def _even_kernel(h_ref, g_ref, wm_ref, wa_ref, gw_ref, gb_ref, on_ref, pw_ref, ps_ref, wo_ref,
                 tril_ref, gsel_ref, lmask_ref, o_ref, state_ref, ext_ref, mix_ref, *, tt):
    t_idx = pl.program_id(1)
    hk = GLA_HEADS * GLA_DK
    hv = GLA_HEADS * GLA_DV

    @pl.when(t_idx == 0)
    def _():
        state_ref[...] = jnp.zeros_like(state_ref)
        ext_ref[0:POOL_HALO, :] = jnp.zeros((POOL_HALO, ext_ref.shape[1]), F32)

    x = h_ref[0]
    hn = _rms(x, g_ref[...]).astype(BF16)
    a_lr = _dot(hn, wa_ref[...])
    a_hi, a_lo = _split_bf16(a_lr, 2)
    z = _dot(a_hi, gw_ref[0]) + (_dot(a_hi, gw_ref[1]) + _dot(a_lo, gw_ref[0])) + gb_ref[...]
    la = _log_sigmoid(z) * (1.0 / GLA_GATE_TAU)
    proj_u = _dot(hn, wm_ref[:, 2 * hk + 2 * hv:])

    bc = _dot_pieces(tril_ref[...], la, 3)
    proj_v = _dot(hn, wm_ref[:, 2 * hk:2 * hk + hv])

    levels = _gla_levels(tt)
    ref_all = _dot_pieces(gsel_ref[...], bc, 2)

    ext_ref[POOL_HALO:POOL_HALO + tt, :] = proj_u
    tpos = (t_idx * tt + lax.broadcasted_iota(jnp.int32, (tt, 1), 0) + 1).astype(F32)
    for gi, w in enumerate(POOL_WINDOWS):
        ls = slice(gi * POOL_GROUP_DIM, (gi + 1) * POOL_GROUP_DIM)
        ug = ext_ref[POOL_HALO:POOL_HALO + tt, ls]
        win = ug
        for dlt in range(1, w):
            win = win + ext_ref[POOL_HALO - dlt:POOL_HALO - dlt + tt, ls]
        pg = win / jnp.minimum(tpos, float(w)) - ug
        pg = _dot(pg.astype(BF16), pw_ref[gi]) * ps_ref[:, ls]
        mix_ref[:, hv + gi * POOL_GROUP_DIM: hv + (gi + 1) * POOL_GROUP_DIM] = pg.astype(BF16)
    ext_ref[0:POOL_HALO, :] = ext_ref[tt:tt + POOL_HALO, :]

    proj_qk = _dot(hn, wm_ref[:, 0:2 * hk])
    proj_g = _dot(hn, wm_ref[:, 2 * hk + hv:2 * hk + 2 * hv])
    rowk = lax.broadcasted_iota(jnp.int32, (tt, hk), 0)
    q_scaled = []
    k_scaled = []
    for li, s in enumerate(levels):
        ref_b = ref_all[li * tt:(li + 1) * tt]
        upper = (rowk & (s - 1)) >= s // 2
        q_scaled.append(jnp.exp(jnp.where(upper, bc - ref_b, 0.0)))
        k_scaled.append(jnp.exp(jnp.where(upper, 0.0, ref_b - bc)))

    e_b = jnp.exp(bc)
    b_last = bc[tt - 1:tt, :]
    e_last = jnp.exp(b_last)
    e_dec = jnp.exp(b_last - bc)

    for hd in range(GLA_HEADS):
        ks = slice(hd * GLA_DK, (hd + 1) * GLA_DK)
        qh = proj_qk[:, ks] * (GLA_DK ** -0.5)
        kh = proj_qk[:, hk + hd * GLA_DK: hk + (hd + 1) * GLA_DK]
        vh = proj_v[:, hd * GLA_DV:(hd + 1) * GLA_DV]
        vh_b = vh.astype(BF16)
        scores = jnp.zeros((tt, tt), F32)
        for li in range(len(levels)):
            qs = (qh * q_scaled[li][:, ks]).astype(BF16)
            kss = (kh * k_scaled[li][:, ks]).astype(BF16)
            scores = scores + jnp.where(lmask_ref[li] > 0.0, _dot_nt(qs, kss), 0.0)
        diag = jnp.sum(qh * kh, axis=-1, keepdims=True)
        st = state_ref[hd]
        inter = _dot_nt((qh * e_b[:, ks]).astype(BF16), st.astype(BF16))
        o_h = inter + _dot(scores.astype(BF16), vh_b) + diag * vh
        state_ref[hd] = st * e_last[:, ks] + _dot_tn(vh_b, (kh * e_dec[:, ks]).astype(BF16))
        gh = proj_g[:, hd * GLA_DV:(hd + 1) * GLA_DV]
        o_n = _rms(o_h, on_ref[:, hd * GLA_DV:(hd + 1) * GLA_DV])
        mix_ref[:, hd * GLA_DV:(hd + 1) * GLA_DV] = (o_n * gh * (1.0 / (1.0 + jnp.exp(-gh)))).astype(BF16)

    o_ref[0] = x + _dot(mix_ref[...], wo_ref[...])


def _even_mixer(h, norm_g, w_in, gate_w, gate_b, out_norm, pool_w, pool_scale, w_out, *, tt=EVEN_TOKEN_TILE):
    b, s, d = h.shape
    hk = GLA_HEADS * GLA_DK
    hv = GLA_HEADS * GLA_DV
    pool_dim = len(POOL_WINDOWS) * POOL_GROUP_DIM
    o_a = 2 * hk + 2 * hv
    w_main = jnp.concatenate([w_in[:, :o_a], w_in[:, o_a + GLA_GATE_RANK:]], axis=1).astype(BF16)
    w_a = jnp.pad(w_in[:, o_a:o_a + GLA_GATE_RANK], ((0, 0), (0, LANES - GLA_GATE_RANK))).astype(BF16)
    gw = jnp.pad(gate_w, ((0, LANES - GLA_GATE_RANK), (0, 0)))
    gw_hi = gw.astype(BF16)
    gw = jnp.stack([gw_hi, (gw - gw_hi.astype(F32)).astype(BF16)])
    tril, gsel, lmask = _gla_constants(tt)
    n_levels = lmask.shape[0]
    n_main = w_main.shape[1]
    mix_dim = hv + pool_dim
    return pl.pallas_call(
        functools.partial(_even_kernel, tt=tt),
        grid=(b, s // tt),
        in_specs=[
            pl.BlockSpec((1, tt, d), lambda bi, ti: (bi, ti, 0)),
            _const_spec((1, d)),
            _const_spec((d, n_main)),
            _const_spec((d, LANES)),
            _const_spec((2, LANES, hk)),
            _const_spec((1, hk)),
            _const_spec((1, hv)),
            _const_spec((len(POOL_WINDOWS), POOL_GROUP_DIM, POOL_GROUP_DIM)),
            _const_spec((1, pool_dim)),
            _const_spec((mix_dim, d)),
            _const_spec((tt, tt)),
            _const_spec((n_levels * tt, tt)),
            _const_spec((n_levels, tt, tt)),
        ],
        out_specs=pl.BlockSpec((1, tt, d), lambda bi, ti: (bi, ti, 0)),
        out_shape=jax.ShapeDtypeStruct((b, s, d), F32),
        scratch_shapes=[
            pltpu.VMEM((GLA_HEADS, GLA_DV, GLA_DK), F32),
            pltpu.VMEM((POOL_HALO + tt, pool_dim), F32),
            pltpu.VMEM((tt, mix_dim), BF16),
        ],
        compiler_params=pltpu.CompilerParams(
            dimension_semantics=("arbitrary", "arbitrary"), vmem_limit_bytes=VMEM_LIMIT),
        name="even_mixer",
    )(h, norm_g.reshape(1, d), w_main, w_a, gw, gate_b.reshape(1, hk), out_norm.reshape(1, hv),
      pool_w.astype(BF16), pool_scale.reshape(1, pool_dim), w_out.astype(BF16), tril, gsel, lmask)


def _odd_proj_kernel(h_ref, g_ref, w_ref, pos_ref, inv_ref, q_ref, k_ref, vt_ref, qi_ref, ki_ref, wi_ref, *, tm):
    nq = ATT_HEADS * ATT_HEAD_DIM
    nkv = ATT_KV_HEADS * ATT_HEAD_DIM
    ni = IDX_HEADS * IDX_DIM
    hn = _rms(h_ref[0], g_ref[...]).astype(BF16)

    def proj_cols(lo, width):
        return _dot(hn, w_ref[:, lo:lo + width])

    per_dot = MXU_WIDTH // ATT_HEAD_DIM
    o_i = nq + 2 * nkv
    xv = proj_cols(nq + nkv, nkv)
    for gi in range(ATT_KV_HEADS):
        vt_ref[0, gi, 0] = xv[:, gi * ATT_HEAD_DIM:(gi + 1) * ATT_HEAD_DIM].T.astype(BF16)
    xqs = [proj_cols(h0 * ATT_HEAD_DIM, MXU_WIDTH) for h0 in range(0, ATT_HEADS, per_dot)]
    xk = proj_cols(nq, nkv)
    xi = proj_cols(o_i, ni)
    tail = proj_cols(o_i + ni, LANES)
    wi_ref[0] = pltpu.roll(tail, LANES - IDX_DIM, 1).T[0:SUBLANES, :]

    pos = pos_ref[0]
    lane = lax.broadcasted_iota(jnp.int32, (tm, LANES), 1)

    half_a = ATT_HEAD_DIM // 2
    half_i = IDX_DIM // 2
    ang = pos * inv_ref[0:1, :]
    cos_p = jnp.cos(ang)
    sin_p = jnp.sin(ang)

    def att_table(t):
        return jnp.where(lane < half_a, t, pltpu.roll(t, half_a, 1))

    def idx_table(t):
        t32 = jnp.where((lane & (2 * half_i - 1)) < half_i, pltpu.roll(t, half_a, 1), pltpu.roll(t, half_a + half_i, 1))
        return jnp.where(lane < IDX_DIM, t32, pltpu.roll(t32, IDX_DIM, 1))

    cos_a = att_table(cos_p)
    sin_a = att_table(sin_p)
    sin_a = jnp.where(lane < half_a, -sin_a, sin_a)

    def rope_att(xh):
        return xh * cos_a + pltpu.roll(xh, ATT_HEAD_DIM // 2, 1) * sin_a

    cos_i = idx_table(cos_p)
    sin_i = idx_table(sin_p)
    low_half = (lane % IDX_DIM) < IDX_DIM // 2

    def rope_idx(xb):
        rot = jnp.where(low_half, -pltpu.roll(xb, LANES - IDX_DIM // 2, 1), pltpu.roll(xb, IDX_DIM // 2, 1))
        return xb * cos_i + rot * sin_i

    for g, xq in enumerate(xqs):
        for hd in range(per_dot):
            xh = xq[:, hd * ATT_HEAD_DIM:(hd + 1) * ATT_HEAD_DIM]
            q_ref[0, g * per_dot + hd] = (rope_att(xh) * (LOG2_E * ATT_HEAD_DIM ** -0.5)).astype(BF16)
    for gi in range(ATT_KV_HEADS):
        xh = xk[:, gi * ATT_HEAD_DIM:(gi + 1) * ATT_HEAD_DIM]
        k_ref[0, :, gi * ATT_HEAD_DIM:(gi + 1) * ATT_HEAD_DIM] = rope_att(xh).astype(BF16)
    for pi in range(ni // LANES):
        xr = rope_idx(xi[:, pi * LANES:(pi + 1) * LANES]).astype(BF16)
        for sub in range(LANES // IDX_DIM):
            qi_ref[0, pi * (LANES // IDX_DIM) + sub] = xr[:, sub * IDX_DIM:(sub + 1) * IDX_DIM]
    ki_ref[0] = rope_idx(tail)[:, :IDX_DIM].astype(BF16)


def _odd_proj(h, norm_g, w_in, positions):
    b, s, d = h.shape
    tm = _dsa_key_chunk(s)
    n_in = w_in.shape[1]
    n_pad = -(-n_in // LANES) * LANES
    w = jnp.pad(w_in, ((0, 0), (0, n_pad - n_in))).astype(BF16)
    inv_a = ROPE_THETA ** (-jnp.arange(0, ATT_HEAD_DIM, 2, dtype=F32) / ATT_HEAD_DIM)
    inv_i = ROPE_THETA ** (-jnp.arange(0, IDX_DIM, 2, dtype=F32) / IDX_DIM)
    inv = jnp.concatenate([inv_a, inv_i])
    inv = jnp.pad(inv[None, :], ((0, SUBLANES - 1), (0, LANES - inv.shape[0])))
    pos = positions.astype(F32).reshape(b, s, 1)
    return pl.pallas_call(
        functools.partial(_odd_proj_kernel, tm=tm),
        grid=(b, s // tm),
        in_specs=[
            pl.BlockSpec((1, tm, d), lambda bi, ti: (bi, ti, 0)),
            _const_spec((1, d)),
            _const_spec((d, n_pad)),
            pl.BlockSpec((1, tm, 1), lambda bi, ti: (bi, ti, 0)),
            _const_spec((SUBLANES, LANES)),
        ],
        out_specs=[
            pl.BlockSpec((1, ATT_HEADS, tm, ATT_HEAD_DIM), lambda bi, ti: (bi, 0, ti, 0)),
            pl.BlockSpec((1, tm, ATT_KV_HEADS * ATT_HEAD_DIM), lambda bi, ti: (bi, ti, 0)),
            pl.BlockSpec((1, ATT_KV_HEADS, 1, ATT_HEAD_DIM, tm), lambda bi, ti: (bi, 0, ti, 0, 0)),
            pl.BlockSpec((1, IDX_HEADS, tm, IDX_DIM), lambda bi, ti: (bi, 0, ti, 0)),
            pl.BlockSpec((1, tm, IDX_DIM), lambda bi, ti: (bi, ti, 0)),
            pl.BlockSpec((1, SUBLANES, tm), lambda bi, ti: (bi, 0, ti)),
        ],
        out_shape=[
            jax.ShapeDtypeStruct((b, ATT_HEADS, s, ATT_HEAD_DIM), BF16),
            jax.ShapeDtypeStruct((b, s, ATT_KV_HEADS * ATT_HEAD_DIM), BF16),
            jax.ShapeDtypeStruct((b, ATT_KV_HEADS, s // tm, ATT_HEAD_DIM, tm), BF16),
            jax.ShapeDtypeStruct((b, IDX_HEADS, s, IDX_DIM), BF16),
            jax.ShapeDtypeStruct((b, s, IDX_DIM), BF16),
            jax.ShapeDtypeStruct((b, SUBLANES, s), F32),
        ],
        compiler_params=pltpu.CompilerParams(
            dimension_semantics=("arbitrary", "arbitrary"), vmem_limit_bytes=VMEM_LIMIT),
        name="odd_proj",
    )(h, norm_g.reshape(1, d), w, pos, inv)


def _float_to_ordered(x):
    bits = lax.bitcast_convert_type(x, jnp.int32)
    return jnp.where(bits >= 0, bits, bits ^ jnp.int32(0x7FFFFFFF))


def _ordered_to_float(o):
    bits = jnp.where(o >= 0, o, o ^ jnp.int32(0x7FFFFFFF))
    return lax.bitcast_convert_type(bits, F32)


def _dsa_kernel(q_ref, qi_ref, wi_ref, k_ref, vt_ref, ki_ref, h_ref, wo_ref, upper_ref, o_ref,
                sc_ref, bias_ref, qe_ref, acc_ref, og_ref, sta_ref, stb_ref, pa_ref, pb_ref, *, tq, kc, top_k):
    qb = pl.program_id(1)
    n_kc = (qb * tq + tq + kc - 1) // kc
    rep = ATT_HEADS // ATT_KV_HEADS
    q_pos = qb * tq + lax.broadcasted_iota(jnp.int32, (1, tq), 1)
    scan_iota = lax.broadcasted_iota(jnp.int32, (SCAN_ROWS, tq), 0)

    qi = qi_ref[0].reshape(IDX_HEADS * tq, IDX_DIM)
    wi = wi_ref[0] * ((IDX_DIM ** -0.5) * (IDX_HEADS ** -0.5))

    def fold(vals, op):
        return op(vals.reshape(vals.shape[0] // COUNT_ROWS, COUNT_ROWS, tq), axis=0)

    def score_body(c, carry):
        mx_p, mn_p = carry
        for r0 in range(0, kc, SCAN_ROWS):
            off = pl.multiple_of(c * kc + r0, SCAN_ROWS)
            lg = jnp.maximum(_dot_nt(ki_ref[0, pl.ds(off, SCAN_ROWS), :], qi), 0.0)
            isc = lg[:, 0:tq] * wi[0:1, :]
            for hd in range(1, IDX_HEADS):
                isc = isc + lg[:, hd * tq:(hd + 1) * tq] * wi[hd:hd + 1, :]
            sc = jnp.where(scan_iota + off <= q_pos, isc, -jnp.inf)
            sc_ref[c, r0:r0 + SCAN_ROWS, :] = sc
            mx_p = jnp.maximum(mx_p, fold(sc, jnp.max))
            mn_p = jnp.minimum(mn_p, fold(isc, jnp.min))
        return mx_p, mn_p

    mx_p, mn_p = lax.fori_loop(0, n_kc, score_body, (jnp.full((COUNT_ROWS, tq), -jnp.inf, F32),
                                                     jnp.full((COUNT_ROWS, tq), jnp.inf, F32)))
    mx = jnp.max(mx_p, axis=0, keepdims=True) + 0.0
    mn = jnp.min(mn_p, axis=0, keepdims=True)

    def count_ge(t):
        def body(c, acc):
            for r0 in range(0, kc, SCAN_ROWS):
                xs = sc_ref[c, r0:r0 + SCAN_ROWS, :].reshape(SCAN_ROWS // COUNT_ROWS, COUNT_ROWS, tq)
                acc = acc + jnp.sum(jnp.where(xs >= t, 1.0, 0.0), axis=0)
            return acc
        part = lax.fori_loop(0, n_kc, body, jnp.zeros((COUNT_ROWS, tq), F32))
        return jnp.sum(part, axis=0, keepdims=True)

    def max_below(t):
        def body(c, acc):
            for r0 in range(0, kc, SCAN_ROWS):
                xs = sc_ref[c, r0:r0 + SCAN_ROWS, :]
                acc = jnp.maximum(acc, fold(jnp.where(xs < t, xs, -jnp.inf), jnp.max))
            return acc
        part = lax.fori_loop(0, n_kc, body, jnp.full((COUNT_ROWS, tq), -jnp.inf, F32))
        return jnp.max(part, axis=0, keepdims=True)

    kf = float(top_k)
    n_adm = (q_pos + 1).astype(F32)
    small = n_adm <= kf
    state = (jnp.where(small, 0.0, mn),
             jnp.where(small, 1.0, _ordered_to_float(_float_to_ordered(mx) + 1)),
             n_adm,
             jnp.where(small, 1.0, 0.0),
             jnp.full((1, tq), -jnp.inf, F32),
             jnp.full((1, tq), kf, F32))

    def bis_body(_, st):
        lo, hi, c_lo, done, thr, c_thr = st
        mid = 0.5 * lo + 0.5 * hi
        c = count_ge(mid)
        active = done == 0.0
        stuck = (mid <= lo) | (mid >= hi)
        fin_stuck = active & stuck
        fin_hit = active & (~stuck) & (c == kf)
        up = active & (~stuck) & (c >= kf)
        down = active & (~stuck) & (c < kf)
        thr = jnp.where(fin_stuck, lo, jnp.where(fin_hit, mid, thr))
        c_thr = jnp.where(fin_stuck, c_lo, c_thr)
        done = jnp.where(fin_stuck | fin_hit, 1.0, done)
        return (jnp.where(up, mid, lo), jnp.where(down, mid, hi), jnp.where(up, c, c_lo), done, thr, c_thr)

    lo, hi, c_lo, done, thr, c_thr = lax.fori_loop(0, BISECT_STEPS, bis_body, state)

    v = max_below(hi)
    c = count_ge(v)
    active = done == 0.0
    ok = active & (c >= kf)
    thr = jnp.where(ok, v, thr)
    c_thr = jnp.where(ok, c, c_thr)
    hi = jnp.where(active & (c < kf), v, hi)
    done = jnp.where(ok, 1.0, done)

    def fine_body(st):
        _, lo_o, hi_o, c_lo, done, thr, c_thr = st
        mid_o = (lo_o >> 1) + (hi_o >> 1) + (lo_o & hi_o & 1)
        mid = _ordered_to_float(mid_o)
        c = count_ge(mid)
        active = done == 0.0
        stuck = mid_o == lo_o
        fin_stuck = active & stuck
        fin_hit = active & (~stuck) & (c == kf)
        up = active & (~stuck) & (c >= kf)
        down = active & (~stuck) & (c < kf)
        thr = jnp.where(fin_stuck, _ordered_to_float(lo_o), jnp.where(fin_hit, mid, thr))
        c_thr = jnp.where(fin_stuck, c_lo, c_thr)
        done = jnp.where(fin_stuck | fin_hit, 1.0, done)
        return (jnp.sum(1.0 - done), jnp.where(up, mid_o, lo_o), jnp.where(down, mid_o, hi_o),
                jnp.where(up, c, c_lo), done, thr, c_thr)

    fine = lax.while_loop(lambda st: st[0] > 0.0, fine_body,
                          (jnp.sum(1.0 - done), _float_to_ordered(lo), _float_to_ordered(hi), c_lo, done, thr, c_thr))
    thr, c_thr = fine[5], fine[6]

    ka = kc // 2
    excess = c_thr - kf
    half_iota = lax.broadcasted_iota(jnp.int32, (ka, tq), 0)

    def mask_body(j, later_total):
        c = n_kc - 1 - j
        halves = []
        for half in (1, 0):
            xs = sc_ref[c, half * ka:(half + 1) * ka, :]
            ties = jnp.where(xs == thr, 1.0, 0.0)
            halves.append((half, xs, ties, _dot(upper_ref[...], ties.astype(BF16))))
        for half, xs, ties, later_in_half in halves:
            later = later_in_half + later_total
            sel = jnp.where(xs > thr, 0.0,
                            jnp.where(ties > 0.0, jnp.where(later >= excess, 0.0, MASKED_SCORE), MASKED_SCORE))
            kidx = half_iota + (c * kc + half * ka)
            bias_ref[c, half * ka:(half + 1) * ka, :] = jnp.where(kidx <= q_pos, sel, MASKED_SCORE).astype(BF16)
            later_total = later[0:1, :] + ties[0:1, :]
        return later_total

    lax.fori_loop(0, n_kc, mask_body, jnp.zeros((1, tq), F32))

    acc_ref[...] = jnp.zeros_like(acc_ref)
    ones_rows = jnp.ones((BF16_SUBLANES, ka), BF16)
    n_qs = tq // LANES
    own_col = ((lax.broadcasted_iota(jnp.int32, (rep * LANES, LANES), 0) & (LANES - 1))
               == lax.broadcasted_iota(jnp.int32, (rep * LANES, LANES), 1))
    for gi in range(ATT_KV_HEADS):
        for sb in range(n_qs):
            qs = q_ref[0, gi * rep:(gi + 1) * rep, sb * LANES:(sb + 1) * LANES, :]
            qe_ref[gi, sb, :, 0:ATT_HEAD_DIM] = qs.reshape(rep * LANES, ATT_HEAD_DIM)
            qe_ref[gi, sb, :, ATT_HEAD_DIM:] = jnp.where(own_col, 1.0, 0.0).astype(BF16)

    def scores_into(dst_ref, c, half):
        off = pl.multiple_of(c * kc + half * ka, ka)
        bias = bias_ref[c, half * ka:(half + 1) * ka, :]
        tops = []
        for gi in range(ATT_KV_HEADS):
            kb = k_ref[0, pl.ds(off, ka), gi * ATT_HEAD_DIM:(gi + 1) * ATT_HEAD_DIM]
            st = jnp.concatenate(
                [_dot_nt(jnp.concatenate([kb, bias[:, sb * LANES:(sb + 1) * LANES]], axis=1), qe_ref[gi, sb])
                 for sb in range(n_qs)], axis=1)
            dst_ref[gi] = st
            tops.append(jnp.max(st, axis=0, keepdims=True))
        return tuple(tops)

    def softmax_into(p_ref, src_ref, m_all, tops):
        m_out, alphas = [], []
        for gi in range(ATT_KV_HEADS):
            m_new = jnp.maximum(m_all[gi], tops[gi])
            alphas.append(jnp.exp2(m_all[gi] - m_new))
            p_ref[gi] = jnp.exp2(src_ref[gi] - m_new).astype(BF16)
            m_out.append(m_new)
        return tuple(m_out), tuple(alphas)

    def values_in(p_ref, c, half, alphas):
        for gi in range(ATT_KV_HEADS):
            vt = jnp.concatenate([vt_ref[0, gi, c, :, half * ka:(half + 1) * ka], ones_rows], axis=0)
            acc_ref[gi] = alphas[gi] * acc_ref[gi] + _dot(vt, p_ref[gi])

    def att_body(c, carry):
        m_all, tops_a, alpha_b = carry
        tops_b = scores_into(stb_ref, c, 1)
        values_in(pb_ref, jnp.maximum(c - 1, 0), 1, alpha_b)
        m_all, alpha_a = softmax_into(pa_ref, sta_ref, m_all, tops_a)
        tops_a = scores_into(sta_ref, jnp.minimum(c + 1, n_kc - 1), 0)
        values_in(pa_ref, c, 0, alpha_a)
        m_all, alpha_b = softmax_into(pb_ref, stb_ref, m_all, tops_b)
        return m_all, tops_a, alpha_b

    pb_ref[...] = jnp.zeros_like(pb_ref)
    m0 = (jnp.full((1, rep * tq), MASKED_SCORE, F32),) * ATT_KV_HEADS
    one = (jnp.ones((1, rep * tq), F32),) * ATT_KV_HEADS
    _, _, alpha_b = lax.fori_loop(0, n_kc, att_body, (m0, scores_into(sta_ref, 0, 0), one))
    values_in(pb_ref, n_kc - 1, 1, alpha_b)
    for gi in range(ATT_KV_HEADS):
        acc = acc_ref[gi]
        og = (acc[0:ATT_HEAD_DIM] / acc[ATT_HEAD_DIM:ATT_HEAD_DIM + 1]).astype(BF16)
        for sb in range(n_qs):
            for r in range(rep):
                hd = gi * rep + r
                col = (sb * rep + r) * LANES
                og_ref[hd * ATT_HEAD_DIM:(hd + 1) * ATT_HEAD_DIM, sb * LANES:(sb + 1) * LANES] = og[:, col:col + LANES]

    o_ref[0] = h_ref[0] + _dot_tn(og_ref[...], wo_ref[...])


def _dsa_key_chunk(s):
    return min(DSA_KEY_CHUNK, s)


def _dsa(h, q, k, v_t, qi, ki, wi_t, w_out):
    b, s, d = h.shape
    tq = DSA_QUERY_TILE
    top_k = min(TOPK_MAX, s // 4)
    kc = _dsa_key_chunk(s)
    assert kc >= top_k and s % kc == 0 and kc % (2 * tq) == 0
    rep = ATT_HEADS // ATT_KV_HEADS
    nkv = ATT_KV_HEADS * ATT_HEAD_DIM
    n_chunks = s // kc
    vrows = ATT_HEAD_DIM + BF16_SUBLANES
    ka = kc // 2
    upper = jnp.asarray(np.arange(ka)[None, :] > np.arange(ka)[:, None], BF16)
    return pl.pallas_call(
        functools.partial(_dsa_kernel, tq=tq, kc=kc, top_k=top_k),
        grid=(b, s // tq),
        in_specs=[
            pl.BlockSpec((1, ATT_HEADS, tq, ATT_HEAD_DIM), lambda bi, qi_: (bi, 0, qi_, 0)),
            pl.BlockSpec((1, IDX_HEADS, tq, IDX_DIM), lambda bi, qi_: (bi, 0, qi_, 0)),
            pl.BlockSpec((1, SUBLANES, tq), lambda bi, qi_: (bi, 0, qi_)),
            pl.BlockSpec((1, s, nkv), lambda bi, qi_: (bi, 0, 0)),
            pl.BlockSpec((1, ATT_KV_HEADS, n_chunks, ATT_HEAD_DIM, kc), lambda bi, qi_: (bi, 0, 0, 0, 0)),
            pl.BlockSpec((1, s, IDX_DIM), lambda bi, qi_: (bi, 0, 0)),
            pl.BlockSpec((1, tq, d), lambda bi, qi_: (bi, qi_, 0)),
            _const_spec((ATT_HEADS * ATT_HEAD_DIM, d)),
            _const_spec((ka, ka)),
        ],
        out_specs=pl.BlockSpec((1, tq, d), lambda bi, qi_: (bi, qi_, 0)),
        out_shape=jax.ShapeDtypeStruct((b, s, d), F32),
        scratch_shapes=[
            pltpu.VMEM((n_chunks, kc, tq), F32),
            pltpu.VMEM((n_chunks, kc, tq), BF16),
            pltpu.VMEM((ATT_KV_HEADS, tq // LANES, rep * LANES, 2 * ATT_HEAD_DIM), BF16),
            pltpu.VMEM((ATT_KV_HEADS, vrows, rep * tq), F32),
            pltpu.VMEM((ATT_HEADS * ATT_HEAD_DIM, tq), BF16),
            pltpu.VMEM((ATT_KV_HEADS, kc // 2, rep * tq), F32),
            pltpu.VMEM((ATT_KV_HEADS, kc // 2, rep * tq), F32),
            pltpu.VMEM((ATT_KV_HEADS, kc // 2, rep * tq), BF16),
            pltpu.VMEM((ATT_KV_HEADS, kc // 2, rep * tq), BF16),
        ],
        compiler_params=pltpu.CompilerParams(
            dimension_semantics=("arbitrary", "arbitrary"), vmem_limit_bytes=VMEM_LIMIT),
        name="dsa",
    )(q, qi, wi_t, k, v_t, ki, h, w_out.astype(BF16), upper)


def kernel(x, positions, ffn1_norm, ffn1_wi, ffn1_wo, mix_norm, ffn2_norm, ffn2_wi, ffn2_wo, even_w_in, gla_gate_w, gla_gate_b, gla_out_norm, pool_w, pool_scale, even_w_out, odd_w_in, odd_w_out, final_norm):
    b, s, d = x.shape
    depth = ffn1_wi.shape[0]
    h = x
    for li in range(depth):
        h = _ffn(h.reshape(b * s, d), ffn1_norm[li], ffn1_wi, ffn1_wo, li, final_norm, final=False)
        h = h.reshape(b, s, d)
        j = li // 2
        if li % 2 == 0:
            h = _even_mixer(h, mix_norm[li], even_w_in[j], gla_gate_w[j], gla_gate_b[j], gla_out_norm[j],
                            pool_w[j], pool_scale[j], even_w_out[j])
        else:
            q, k, v_t, qi, ki, wi_t = _odd_proj(h, mix_norm[li], odd_w_in[j], positions)
            h = _dsa(h, q, k, v_t, qi, ki, wi_t, odd_w_out[j])
        last = li == depth - 1
        h = _ffn(h.reshape(b * s, d), ffn2_norm[li], ffn2_wi, ffn2_wo, li, final_norm, final=last)
        h = h.reshape(b, s, d)
    return h
```

```python
import functools

import jax
import jax.numpy as jnp
import numpy as np
from jax import lax
from jax.experimental import pallas as pl
from jax.experimental.pallas import tpu as pltpu

F32 = jnp.float32
BF16 = jnp.bfloat16

RMS_EPS = 1e-6
ROPE_THETA = 10000.0

GLA_HEADS = 4
GLA_DK = 64
GLA_DV = 128
GLA_GATE_RANK = 16
GLA_GATE_TAU = 16.0

POOL_WINDOWS = (2, 4, 8, 16)
POOL_GROUP_DIM = 128
POOL_HALO = 16

ATT_HEADS = 8
ATT_KV_HEADS = 2
ATT_HEAD_DIM = 128
IDX_HEADS = 4
IDX_DIM = 64
TOPK_MAX = 256
DSA_QUERY_TILE = 256

LANES = 128
SUBLANES = 8
BF16_SUBLANES = 16
FFN_TOKEN_TILE = 512
EVEN_TOKEN_TILE = 256
MXU_WIDTH = 256
MASKED_SCORE = -1e30
SCAN_ROWS = 128
COUNT_ROWS = 32
BISECT_STEPS = 16
DSA_KEY_CHUNK = 512
LOG2_E = 1.4426950408889634
VMEM_LIMIT = 56 * 1024 * 1024


def _rms(x, g):
    ms = jnp.mean(x * x, axis=-1, keepdims=True)
    return x * lax.rsqrt(ms + RMS_EPS) * g


def _dot(a, b):
    return jnp.dot(a, b, preferred_element_type=F32)


def _dot_nt(a, b):
    return lax.dot_general(a, b, (((1,), (1,)), ((), ())), preferred_element_type=F32)


def _dot_tn(a, b):
    return lax.dot_general(a, b, (((0,), (0,)), ((), ())), preferred_element_type=F32)


def _const_spec(shape):
    n = len(shape)
    return pl.BlockSpec(shape, lambda *_: (0,) * n)


def _ffn_kernel(h_ref, g_ref, wi_ref, wo_ref, fg_ref, o_ref, xn_ref, acc_ref, *, f, tf, final):
    x = h_ref[...]
    xn_ref[...] = _rms(x, g_ref[...]).astype(BF16)
    for c in range(f // tf):
        xn = xn_ref[...]
        gate = _dot(xn, wi_ref[0, :, c * tf:(c + 1) * tf].astype(BF16))
        up = _dot(xn, wi_ref[0, :, f + c * tf:f + (c + 1) * tf].astype(BF16))
        act = gate * (1.0 / (1.0 + jnp.exp(-gate))) * up
        y = _dot(act.astype(BF16), wo_ref[0, c * tf:(c + 1) * tf, :].astype(BF16))
        if c == 0:
            acc_ref[...] = y
        else:
            acc_ref[...] += y
    out = x + 0.5 * acc_ref[...]
    if final:
        out = _rms(out, fg_ref[...])
    o_ref[...] = out


def _ffn(h, norm_g, wi_all, wo_all, layer, final_g, *, final, tm=FFN_TOKEN_TILE, tf=MXU_WIDTH):
    n, d = h.shape
    f = wo_all.shape[1]
    assert f % tf == 0 and n % tm == 0
    return pl.pallas_call(
        functools.partial(_ffn_kernel, f=f, tf=tf, final=final),
        grid=(n // tm,),
        in_specs=[
            pl.BlockSpec((tm, d), lambda i: (i, 0)),
            _const_spec((1, d)),
            pl.BlockSpec((1, d, 2 * f), lambda i: (layer, 0, 0), pipeline_mode=pl.Buffered(1)),
            pl.BlockSpec((1, f, d), lambda i: (layer, 0, 0), pipeline_mode=pl.Buffered(1)),
            _const_spec((1, d)),
        ],
        out_specs=pl.BlockSpec((tm, d), lambda i: (i, 0)),
        out_shape=jax.ShapeDtypeStruct((n, d), F32),
        scratch_shapes=[pltpu.VMEM((tm, d), BF16), pltpu.VMEM((tm, d), F32)],
        compiler_params=pltpu.CompilerParams(
            dimension_semantics=("arbitrary",), vmem_limit_bytes=VMEM_LIMIT),
        name="ffn_final" if final else "ffn",
    )(h, norm_g.reshape(1, d), wi_all, wo_all, final_g.reshape(1, d))


def _log_sigmoid(z):
    return jnp.minimum(z, 0.0) - jnp.log1p(jnp.exp(-jnp.abs(z)))


def _split_bf16(x, n):
    pieces = []
    for _ in range(n - 1):
        hi = x.astype(BF16)
        pieces.append(hi)
        x = x - hi.astype(F32)
    pieces.append(x.astype(BF16))
    return pieces


def _dot_pieces(sel, x, n):
    pieces = _split_bf16(x, n)
    out = _dot(sel, pieces[0])
    for piece in pieces[1:]:
        out = out + _dot(sel, piece)
    return out


def _gla_levels(tt):
    return [tt >> i for i in range(tt.bit_length() - 1)]


def _gla_constants(tt):
    levels = _gla_levels(tt)
    row = np.arange(tt)[:, None]
    col = np.arange(tt)[None, :]
    tril = (col <= row).astype(np.float32)
    gsel = np.concatenate([(col == (row // s) * s + s // 2 - 1).astype(np.float32) for s in levels], axis=0)
    lmask = np.stack([((row // s == col // s) & (row % s >= s // 2) & (col % s < s // 2)).astype(np.float32)
                      for s in levels])
    return jnp.asarray(tril, BF16), jnp.asarray(gsel, BF16), jnp.asarray(lmask, F32)


def _even_kernel(h_ref, g_ref, wm_ref, wa_ref, gw_ref, gb_ref, on_ref, pw_ref, ps_ref, wo_ref,
                 tril_ref, gsel_ref, lmask_ref, o_ref, state_ref, ext_ref, mix_ref, *, tt):
    t_idx = pl.program_id(1)
    hk = GLA_HEADS * GLA_DK
    hv = GLA_HEADS * GLA_DV

    @pl.when(t_idx == 0)
    def _():
        state_ref[...] = jnp.zeros_like(state_ref)
        ext_ref[0:POOL_HALO, :] = jnp.zeros((POOL_HALO, ext_ref.shape[1]), F32)

    x = h_ref[0]
    hn = _rms(x, g_ref[...]).astype(BF16)
    a_lr = _dot(hn, wa_ref[...])
    a_hi, a_lo = _split_bf16(a_lr, 2)
    z = _dot(a_hi, gw_ref[0]) + (_dot(a_hi, gw_ref[1]) + _dot(a_lo, gw_ref[0])) + gb_ref[...]
    la = _log_sigmoid(z) * (1.0 / GLA_GATE_TAU)
    proj_u = _dot(hn, wm_ref[:, 2 * hk + 2 * hv:])

    bc = _dot_pieces(tril_ref[...], la, 3)
    proj_v = _dot(hn, wm_ref[:, 2 * hk:2 * hk + hv])

    levels = _gla_levels(tt)
    ref_all = _dot_pieces(gsel_ref[...], bc, 2)

    ext_ref[POOL_HALO:POOL_HALO + tt, :] = proj_u
    tpos = (t_idx * tt + lax.broadcasted_iota(jnp.int32, (tt, 1), 0) + 1).astype(F32)
    for gi, w in enumerate(POOL_WINDOWS):
        ls = slice(gi * POOL_GROUP_DIM, (gi + 1) * POOL_GROUP_DIM)
        ug = ext_ref[POOL_HALO:POOL_HALO + tt, ls]
        win = ug
        for dlt in range(1, w):
            win = win + ext_ref[POOL_HALO - dlt:POOL_HALO - dlt + tt, ls]
        pg = win / jnp.minimum(tpos, float(w)) - ug
        pg = _dot(pg.astype(BF16), pw_ref[gi]) * ps_ref[:, ls]
        mix_ref[:, hv + gi * POOL_GROUP_DIM: hv + (gi + 1) * POOL_GROUP_DIM] = pg.astype(BF16)
    ext_ref[0:POOL_HALO, :] = ext_ref[tt:tt + POOL_HALO, :]

    proj_qk = _dot(hn, wm_ref[:, 0:2 * hk])
    proj_g = _dot(hn, wm_ref[:, 2 * hk + hv:2 * hk + 2 * hv])
    rowk = lax.broadcasted_iota(jnp.int32, (tt, hk), 0)
    q_scaled = []
    k_scaled = []
    for li, s in enumerate(levels):
        ref_b = ref_all[li * tt:(li + 1) * tt]
        upper = (rowk & (s - 1)) >= s // 2
        q_scaled.append(jnp.exp(jnp.where(upper, bc - ref_b, 0.0)))
        k_scaled.append(jnp.exp(jnp.where(upper, 0.0, ref_b - bc)))

    e_b = jnp.exp(bc)
    b_last = bc[tt - 1:tt, :]
    e_last = jnp.exp(b_last)
    e_dec = jnp.exp(b_last - bc)

    for hd in range(GLA_HEADS):
        ks = slice(hd * GLA_DK, (hd + 1) * GLA_DK)
        qh = proj_qk[:, ks] * (GLA_DK ** -0.5)
        kh = proj_qk[:, hk + hd * GLA_DK: hk + (hd + 1) * GLA_DK]
        vh = proj_v[:, hd * GLA_DV:(hd + 1) * GLA_DV]
        vh_b = vh.astype(BF16)
        scores = jnp.zeros((tt, tt), F32)
        for li in range(len(levels)):
            qs = (qh * q_scaled[li][:, ks]).astype(BF16)
            kss = (kh * k_scaled[li][:, ks]).astype(BF16)
            scores = scores + jnp.where(lmask_ref[li] > 0.0, _dot_nt(qs, kss), 0.0)
        diag = jnp.sum(qh * kh, axis=-1, keepdims=True)
        st = state_ref[hd]
        inter = _dot_nt((qh * e_b[:, ks]).astype(BF16), st.astype(BF16))
        o_h = inter + _dot(scores.astype(BF16), vh_b) + diag * vh
        state_ref[hd] = st * e_last[:, ks] + _dot_tn(vh_b, (kh * e_dec[:, ks]).astype(BF16))
        gh = proj_g[:, hd * GLA_DV:(hd + 1) * GLA_DV]
        o_n = _rms(o_h, on_ref[:, hd * GLA_DV:(hd + 1) * GLA_DV])
        mix_ref[:, hd * GLA_DV:(hd + 1) * GLA_DV] = (o_n * gh * (1.0 / (1.0 + jnp.exp(-gh)))).astype(BF16)

    o_ref[0] = x + _dot(mix_ref[...], wo_ref[...])


def _even_mixer(h, norm_g, w_in, gate_w, gate_b, out_norm, pool_w, pool_scale, w_out, *, tt=EVEN_TOKEN_TILE):
    b, s, d = h.shape
    hk = GLA_HEADS * GLA_DK
    hv = GLA_HEADS * GLA_DV
    pool_dim = len(POOL_WINDOWS) * POOL_GROUP_DIM
    o_a = 2 * hk + 2 * hv
    w_main = jnp.concatenate([w_in[:, :o_a], w_in[:, o_a + GLA_GATE_RANK:]], axis=1).astype(BF16)
    w_a = jnp.pad(w_in[:, o_a:o_a + GLA_GATE_RANK], ((0, 0), (0, LANES - GLA_GATE_RANK))).astype(BF16)
    gw = jnp.pad(gate_w, ((0, LANES - GLA_GATE_RANK), (0, 0)))
    gw_hi = gw.astype(BF16)
    gw = jnp.stack([gw_hi, (gw - gw_hi.astype(F32)).astype(BF16)])
    tril, gsel, lmask = _gla_constants(tt)
    n_levels = lmask.shape[0]
    n_main = w_main.shape[1]
    mix_dim = hv + pool_dim
    return pl.pallas_call(
        functools.partial(_even_kernel, tt=tt),
        grid=(b, s // tt),
        in_specs=[
            pl.BlockSpec((1, tt, d), lambda bi, ti: (bi, ti, 0)),
            _const_spec((1, d)),
            _const_spec((d, n_main)),
            _const_spec((d, LANES)),
            _const_spec((2, LANES, hk)),
            _const_spec((1, hk)),
            _const_spec((1, hv)),
            _const_spec((len(POOL_WINDOWS), POOL_GROUP_DIM, POOL_GROUP_DIM)),
            _const_spec((1, pool_dim)),
            _const_spec((mix_dim, d)),
            _const_spec((tt, tt)),
            _const_spec((n_levels * tt, tt)),
            _const_spec((n_levels, tt, tt)),
        ],
        out_specs=pl.BlockSpec((1, tt, d), lambda bi, ti: (bi, ti, 0)),
        out_shape=jax.ShapeDtypeStruct((b, s, d), F32),
        scratch_shapes=[
            pltpu.VMEM((GLA_HEADS, GLA_DV, GLA_DK), F32),
            pltpu.VMEM((POOL_HALO + tt, pool_dim), F32),
            pltpu.VMEM((tt, mix_dim), BF16),
        ],
        compiler_params=pltpu.CompilerParams(
            dimension_semantics=("arbitrary", "arbitrary"), vmem_limit_bytes=VMEM_LIMIT),
        name="even_mixer",
    )(h, norm_g.reshape(1, d), w_main, w_a, gw, gate_b.reshape(1, hk), out_norm.reshape(1, hv),
      pool_w.astype(BF16), pool_scale.reshape(1, pool_dim), w_out.astype(BF16), tril, gsel, lmask)


def _odd_proj_kernel(h_ref, g_ref, w_ref, pos_ref, inv_ref, q_ref, k_ref, vt_ref, qi_ref, ki_ref, wi_ref, *, tm):
    nq = ATT_HEADS * ATT_HEAD_DIM
    nkv = ATT_KV_HEADS * ATT_HEAD_DIM
    ni = IDX_HEADS * IDX_DIM
    hn = _rms(h_ref[0], g_ref[...]).astype(BF16)

    def proj_cols(lo, width):
        return _dot(hn, w_ref[:, lo:lo + width])

    per_dot = MXU_WIDTH // ATT_HEAD_DIM
    o_i = nq + 2 * nkv
    xv = proj_cols(nq + nkv, nkv)
    for gi in range(ATT_KV_HEADS):
        vt_ref[0, gi, 0] = xv[:, gi * ATT_HEAD_DIM:(gi + 1) * ATT_HEAD_DIM].T.astype(BF16)
    xqs = [proj_cols(h0 * ATT_HEAD_DIM, MXU_WIDTH) for h0 in range(0, ATT_HEADS, per_dot)]
    xk = proj_cols(nq, nkv)
    xi = proj_cols(o_i, ni)
    tail = proj_cols(o_i + ni, LANES)
    wi_ref[0] = pltpu.roll(tail, LANES - IDX_DIM, 1).T[0:SUBLANES, :]

    pos = pos_ref[0]
    lane = lax.broadcasted_iota(jnp.int32, (tm, LANES), 1)

    half_a = ATT_HEAD_DIM // 2
    half_i = IDX_DIM // 2
    ang = pos * inv_ref[0:1, :]
    cos_p = jnp.cos(ang)
    sin_p = jnp.sin(ang)

    def att_table(t):
        return jnp.where(lane < half_a, t, pltpu.roll(t, half_a, 1))

    def idx_table(t):
        t32 = jnp.where((lane & (2 * half_i - 1)) < half_i, pltpu.roll(t, half_a, 1), pltpu.roll(t, half_a + half_i, 1))
        return jnp.where(lane < IDX_DIM, t32, pltpu.roll(t32, IDX_DIM, 1))

    cos_a = att_table(cos_p)
    sin_a = att_table(sin_p)
    sin_a = jnp.where(lane < half_a, -sin_a, sin_a)

    def rope_att(xh):
        return xh * cos_a + pltpu.roll(xh, ATT_HEAD_DIM // 2, 1) * sin_a

    cos_i = idx_table(cos_p)
    sin_i = idx_table(sin_p)
    low_half = (lane % IDX_DIM) < IDX_DIM // 2

    def rope_idx(xb):
        rot = jnp.where(low_half, -pltpu.roll(xb, LANES - IDX_DIM // 2, 1), pltpu.roll(xb, IDX_DIM // 2, 1))
        return xb * cos_i + rot * sin_i

    for g, xq in enumerate(xqs):
        for hd in range(per_dot):
            xh = xq[:, hd * ATT_HEAD_DIM:(hd + 1) * ATT_HEAD_DIM]
            q_ref[0, g * per_dot + hd] = (rope_att(xh) * (LOG2_E * ATT_HEAD_DIM ** -0.5)).astype(BF16)
    for gi in range(ATT_KV_HEADS):
        xh = xk[:, gi * ATT_HEAD_DIM:(gi + 1) * ATT_HEAD_DIM]
        k_ref[0, :, gi * ATT_HEAD_DIM:(gi + 1) * ATT_HEAD_DIM] = rope_att(xh).astype(BF16)
    for pi in range(ni // LANES):
        xr = rope_idx(xi[:, pi * LANES:(pi + 1) * LANES]).astype(BF16)
        for sub in range(LANES // IDX_DIM):
            qi_ref[0, pi * (LANES // IDX_DIM) + sub] = xr[:, sub * IDX_DIM:(sub + 1) * IDX_DIM]
    ki_ref[0] = rope_idx(tail)[:, :IDX_DIM].astype(BF16)


def _odd_proj(h, norm_g, w_in, positions):
    b, s, d = h.shape
    tm = _dsa_key_chunk(s)
    n_in = w_in.shape[1]
    n_pad = -(-n_in // LANES) * LANES
    w = jnp.pad(w_in, ((0, 0), (0, n_pad - n_in))).astype(BF16)
    inv_a = ROPE_THETA ** (-jnp.arange(0, ATT_HEAD_DIM, 2, dtype=F32) / ATT_HEAD_DIM)
    inv_i = ROPE_THETA ** (-jnp.arange(0, IDX_DIM, 2, dtype=F32) / IDX_DIM)
    inv = jnp.concatenate([inv_a, inv_i])
    inv = jnp.pad(inv[None, :], ((0, SUBLANES - 1), (0, LANES - inv.shape[0])))
    pos = positions.astype(F32).reshape(b, s, 1)
    return pl.pallas_call(
        functools.partial(_odd_proj_kernel, tm=tm),
        grid=(b, s // tm),
        in_specs=[
            pl.BlockSpec((1, tm, d), lambda bi, ti: (bi, ti, 0)),
            _const_spec((1, d)),
            _const_spec((d, n_pad)),
            pl.BlockSpec((1, tm, 1), lambda bi, ti: (bi, ti, 0)),
            _const_spec((SUBLANES, LANES)),
        ],
        out_specs=[
            pl.BlockSpec((1, ATT_HEADS, tm, ATT_HEAD_DIM), lambda bi, ti: (bi, 0, ti, 0)),
            pl.BlockSpec((1, tm, ATT_KV_HEADS * ATT_HEAD_DIM), lambda bi, ti: (bi, ti, 0)),
            pl.BlockSpec((1, ATT_KV_HEADS, 1, ATT_HEAD_DIM, tm), lambda bi, ti: (bi, 0, ti, 0, 0)),
            pl.BlockSpec((1, IDX_HEADS, tm, IDX_DIM), lambda bi, ti: (bi, 0, ti, 0)),
            pl.BlockSpec((1, tm, IDX_DIM), lambda bi, ti: (bi, ti, 0)),
            pl.BlockSpec((1, SUBLANES, tm), lambda bi, ti: (bi, 0, ti)),
        ],
        out_shape=[
            jax.ShapeDtypeStruct((b, ATT_HEADS, s, ATT_HEAD_DIM), BF16),
            jax.ShapeDtypeStruct((b, s, ATT_KV_HEADS * ATT_HEAD_DIM), BF16),
            jax.ShapeDtypeStruct((b, ATT_KV_HEADS, s // tm, ATT_HEAD_DIM, tm), BF16),
            jax.ShapeDtypeStruct((b, IDX_HEADS, s, IDX_DIM), BF16),
            jax.ShapeDtypeStruct((b, s, IDX_DIM), BF16),
            jax.ShapeDtypeStruct((b, SUBLANES, s), F32),
        ],
        compiler_params=pltpu.CompilerParams(
            dimension_semantics=("arbitrary", "arbitrary"), vmem_limit_bytes=VMEM_LIMIT),
        name="odd_proj",
    )(h, norm_g.reshape(1, d), w, pos, inv)


def _float_to_ordered(x):
    bits = lax.bitcast_convert_type(x, jnp.int32)
    return jnp.where(bits >= 0, bits, bits ^ jnp.int32(0x7FFFFFFF))


def _ordered_to_float(o):
    bits = jnp.where(o >= 0, o, o ^ jnp.int32(0x7FFFFFFF))
    return lax.bitcast_convert_type(bits, F32)


def _dsa_kernel(q_ref, qi_ref, wi_ref, k_ref, vt_ref, ki_ref, h_ref, wo_ref, upper_ref, o_ref,
                sc_ref, bias_ref, qe_ref, acc_ref, og_ref, sta_ref, stb_ref, pa_ref, pb_ref, *, tq, kc, top_k):
    qb = pl.program_id(1)
    n_kc = (qb * tq + tq + kc - 1) // kc
    rep = ATT_HEADS // ATT_KV_HEADS
    q_pos = qb * tq + lax.broadcasted_iota(jnp.int32, (1, tq), 1)
    scan_iota = lax.broadcasted_iota(jnp.int32, (SCAN_ROWS, tq), 0)

    qi = qi_ref[0].reshape(IDX_HEADS * tq, IDX_DIM)
    wi = wi_ref[0] * ((IDX_DIM ** -0.5) * (IDX_HEADS ** -0.5))

    def fold(vals, op):
        return op(vals.reshape(vals.shape[0] // COUNT_ROWS, COUNT_ROWS, tq), axis=0)

    def score_body(c, carry):
        mx_p, mn_p = carry
        for r0 in range(0, kc, SCAN_ROWS):
            off = pl.multiple_of(c * kc + r0, SCAN_ROWS)
            lg = jnp.maximum(_dot_nt(ki_ref[0, pl.ds(off, SCAN_ROWS), :], qi), 0.0)
            isc = lg[:, 0:tq] * wi[0:1, :]
            for hd in range(1, IDX_HEADS):
                isc = isc + lg[:, hd * tq:(hd + 1) * tq] * wi[hd:hd + 1, :]
            sc = jnp.where(scan_iota + off <= q_pos, isc, -jnp.inf)
            sc_ref[c, r0:r0 + SCAN_ROWS, :] = sc
            mx_p = jnp.maximum(mx_p, fold(sc, jnp.max))
            mn_p = jnp.minimum(mn_p, fold(isc, jnp.min))
        return mx_p, mn_p

    mx_p, mn_p = lax.fori_loop(0, n_kc, score_body, (jnp.full((COUNT_ROWS, tq), -jnp.inf, F32),
                                                     jnp.full((COUNT_ROWS, tq), jnp.inf, F32)))
    mx = jnp.max(mx_p, axis=0, keepdims=True) + 0.0
    mn = jnp.min(mn_p, axis=0, keepdims=True)

    def count_ge(t):
        def body(c, acc):
            for r0 in range(0, kc, SCAN_ROWS):
                xs = sc_ref[c, r0:r0 + SCAN_ROWS, :].reshape(SCAN_ROWS // COUNT_ROWS, COUNT_ROWS, tq)
                acc = acc + jnp.sum(jnp.where(xs >= t, 1.0, 0.0), axis=0)
            return acc
        part = lax.fori_loop(0, n_kc, body, jnp.zeros((COUNT_ROWS, tq), F32))
        return jnp.sum(part, axis=0, keepdims=True)

    def max_below(t):
        def body(c, acc):
            for r0 in range(0, kc, SCAN_ROWS):
                xs = sc_ref[c, r0:r0 + SCAN_ROWS, :]
                acc = jnp.maximum(acc, fold(jnp.where(xs < t, xs, -jnp.inf), jnp.max))
            return acc
        part = lax.fori_loop(0, n_kc, body, jnp.full((COUNT_ROWS, tq), -jnp.inf, F32))
        return jnp.max(part, axis=0, keepdims=True)

    kf = float(top_k)
    n_adm = (q_pos + 1).astype(F32)
    small = n_adm <= kf
    state = (jnp.where(small, 0.0, mn),
             jnp.where(small, 1.0, _ordered_to_float(_float_to_ordered(mx) + 1)),
             n_adm,
             jnp.where(small, 1.0, 0.0),
             jnp.full((1, tq), -jnp.inf, F32),
             jnp.full((1, tq), kf, F32))

    def bis_body(_, st):
        lo, hi, c_lo, done, thr, c_thr = st
        mid = 0.5 * lo + 0.5 * hi
        c = count_ge(mid)
        active = done == 0.0
        stuck = (mid <= lo) | (mid >= hi)
        fin_stuck = active & stuck
        fin_hit = active & (~stuck) & (c == kf)
        up = active & (~stuck) & (c >= kf)
        down = active & (~stuck) & (c < kf)
        thr = jnp.where(fin_stuck, lo, jnp.where(fin_hit, mid, thr))
        c_thr = jnp.where(fin_stuck, c_lo, c_thr)
        done = jnp.where(fin_stuck | fin_hit, 1.0, done)
        return (jnp.where(up, mid, lo), jnp.where(down, mid, hi), jnp.where(up, c, c_lo), done, thr, c_thr)

    _, hi, _, done, thr, c_thr = lax.fori_loop(0, BISECT_STEPS, bis_body, state)

    def peel_body(st):
        _, hi, done, thr, c_thr = st
        v = max_below(hi)
        c = count_ge(v)
        active = done == 0.0
        ok = active & (c >= kf)
        done = jnp.where(ok, 1.0, done)
        return (jnp.sum(1.0 - done), jnp.where(active & (c < kf), v, hi), done,
                jnp.where(ok, v, thr), jnp.where(ok, c, c_thr))

    _, _, _, thr, c_thr = lax.while_loop(lambda st: st[0] > 0.0, peel_body,
                                         (jnp.sum(1.0 - done), hi, done, thr, c_thr))

    ka = kc // 2
    excess = c_thr - kf
    half_iota = lax.broadcasted_iota(jnp.int32, (ka, tq), 0)

    def mask_body(j, later_total):
        c = n_kc - 1 - j
        halves = []
        for half in (1, 0):
            xs = sc_ref[c, half * ka:(half + 1) * ka, :]
            ties = jnp.where(xs == thr, 1.0, 0.0)
            halves.append((half, xs, ties, _dot(upper_ref[...], ties.astype(BF16))))
        for half, xs, ties, later_in_half in halves:
            later = later_in_half + later_total
            sel = jnp.where(xs > thr, 0.0,
                            jnp.where(ties > 0.0, jnp.where(later >= excess, 0.0, MASKED_SCORE), MASKED_SCORE))
            kidx = half_iota + (c * kc + half * ka)
            bias_ref[c, half * ka:(half + 1) * ka, :] = jnp.where(kidx <= q_pos, sel, MASKED_SCORE).astype(BF16)
            later_total = later[0:1, :] + ties[0:1, :]
        return later_total

    lax.fori_loop(0, n_kc, mask_body, jnp.zeros((1, tq), F32))

    acc_ref[...] = jnp.zeros_like(acc_ref)
    ones_rows = jnp.ones((BF16_SUBLANES, ka), BF16)
    n_qs = tq // LANES
    own_col = ((lax.broadcasted_iota(jnp.int32, (rep * LANES, LANES), 0) & (LANES - 1))
               == lax.broadcasted_iota(jnp.int32, (rep * LANES, LANES), 1))
    for gi in range(ATT_KV_HEADS):
        for sb in range(n_qs):
            qs = q_ref[0, gi * rep:(gi + 1) * rep, sb * LANES:(sb + 1) * LANES, :]
            qe_ref[gi, sb, :, 0:ATT_HEAD_DIM] = qs.reshape(rep * LANES, ATT_HEAD_DIM)
            qe_ref[gi, sb, :, ATT_HEAD_DIM:] = jnp.where(own_col, 1.0, 0.0).astype(BF16)

    def scores_into(dst_ref, c, half):
        off = pl.multiple_of(c * kc + half * ka, ka)
        bias = bias_ref[c, half * ka:(half + 1) * ka, :]
        tops = []
        for gi in range(ATT_KV_HEADS):
            kb = k_ref[0, pl.ds(off, ka), gi * ATT_HEAD_DIM:(gi + 1) * ATT_HEAD_DIM]
            st = jnp.concatenate(
                [_dot_nt(jnp.concatenate([kb, bias[:, sb * LANES:(sb + 1) * LANES]], axis=1), qe_ref[gi, sb])
                 for sb in range(n_qs)], axis=1)
            dst_ref[gi] = st
            tops.append(jnp.max(st, axis=0, keepdims=True))
        return tuple(tops)

    def softmax_into(p_ref, src_ref, m_all, tops):
        m_out, alphas = [], []
        for gi in range(ATT_KV_HEADS):
            m_new = jnp.maximum(m_all[gi], tops[gi])
            alphas.append(jnp.exp2(m_all[gi] - m_new))
            p_ref[gi] = jnp.exp2(src_ref[gi] - m_new).astype(BF16)
            m_out.append(m_new)
        return tuple(m_out), tuple(alphas)

    def values_in(p_ref, c, half, alphas):
        for gi in range(ATT_KV_HEADS):
            vt = jnp.concatenate([vt_ref[0, gi, c, :, half * ka:(half + 1) * ka], ones_rows], axis=0)
            acc_ref[gi] = alphas[gi] * acc_ref[gi] + _dot(vt, p_ref[gi])

    def att_body(c, carry):
        m_all, tops_a, alpha_b = carry
        tops_b = scores_into(stb_ref, c, 1)
        values_in(pb_ref, jnp.maximum(c - 1, 0), 1, alpha_b)
        m_all, alpha_a = softmax_into(pa_ref, sta_ref, m_all, tops_a)
        tops_a = scores_into(sta_ref, jnp.minimum(c + 1, n_kc - 1), 0)
        values_in(pa_ref, c, 0, alpha_a)
        m_all, alpha_b = softmax_into(pb_ref, stb_ref, m_all, tops_b)
        return m_all, tops_a, alpha_b

    pb_ref[...] = jnp.zeros_like(pb_ref)
    m0 = (jnp.full((1, rep * tq), MASKED_SCORE, F32),) * ATT_KV_HEADS
    one = (jnp.ones((1, rep * tq), F32),) * ATT_KV_HEADS
    _, _, alpha_b = lax.fori_loop(0, n_kc, att_body, (m0, scores_into(sta_ref, 0, 0), one))
    values_in(pb_ref, n_kc - 1, 1, alpha_b)
    for gi in range(ATT_KV_HEADS):
        acc = acc_ref[gi]
        og = (acc[0:ATT_HEAD_DIM] / acc[ATT_HEAD_DIM:ATT_HEAD_DIM + 1]).astype(BF16)
        for sb in range(n_qs):
            for r in range(rep):
                hd = gi * rep + r
                col = (sb * rep + r) * LANES
                og_ref[hd * ATT_HEAD_DIM:(hd + 1) * ATT_HEAD_DIM, sb * LANES:(sb + 1) * LANES] = og[:, col:col + LANES]

    o_ref[0] = h_ref[0] + _dot_tn(og_ref[...], wo_ref[...])


def _dsa_key_chunk(s):
    return min(DSA_KEY_CHUNK, s)


def _dsa(h, q, k, v_t, qi, ki, wi_t, w_out):
    b, s, d = h.shape
    tq = DSA_QUERY_TILE
    top_k = min(TOPK_MAX, s // 4)
    kc = _dsa_key_chunk(s)
    assert kc >= top_k and s % kc == 0 and kc % (2 * tq) == 0
    rep = ATT_HEADS // ATT_KV_HEADS
    nkv = ATT_KV_HEADS * ATT_HEAD_DIM
    n_chunks = s // kc
    vrows = ATT_HEAD_DIM + BF16_SUBLANES
    ka = kc // 2
    upper = jnp.asarray(np.arange(ka)[None, :] > np.arange(ka)[:, None], BF16)
    return pl.pallas_call(
        functools.partial(_dsa_kernel, tq=tq, kc=kc, top_k=top_k),
        grid=(b, s // tq),
        in_specs=[
            pl.BlockSpec((1, ATT_HEADS, tq, ATT_HEAD_DIM), lambda bi, qi_: (bi, 0, qi_, 0)),
            pl.BlockSpec((1, IDX_HEADS, tq, IDX_DIM), lambda bi, qi_: (bi, 0, qi_, 0)),
            pl.BlockSpec((1, SUBLANES, tq), lambda bi, qi_: (bi, 0, qi_)),
            pl.BlockSpec((1, s, nkv), lambda bi, qi_: (bi, 0, 0)),
            pl.BlockSpec((1, ATT_KV_HEADS, n_chunks, ATT_HEAD_DIM, kc), lambda bi, qi_: (bi, 0, 0, 0, 0)),
            pl.BlockSpec((1, s, IDX_DIM), lambda bi, qi_: (bi, 0, 0)),
            pl.BlockSpec((1, tq, d), lambda bi, qi_: (bi, qi_, 0)),
            _const_spec((ATT_HEADS * ATT_HEAD_DIM, d)),
            _const_spec((ka, ka)),
        ],
        out_specs=pl.BlockSpec((1, tq, d), lambda bi, qi_: (bi, qi_, 0)),
        out_shape=jax.ShapeDtypeStruct((b, s, d), F32),
        scratch_shapes=[
            pltpu.VMEM((n_chunks, kc, tq), F32),
            pltpu.VMEM((n_chunks, kc, tq), BF16),
            pltpu.VMEM((ATT_KV_HEADS, tq // LANES, rep * LANES, 2 * ATT_HEAD_DIM), BF16),
            pltpu.VMEM((ATT_KV_HEADS, vrows, rep * tq), F32),
            pltpu.VMEM((ATT_HEADS * ATT_HEAD_DIM, tq), BF16),
            pltpu.VMEM((ATT_KV_HEADS, kc // 2, rep * tq), F32),
            pltpu.VMEM((ATT_KV_HEADS, kc // 2, rep * tq), F32),
            pltpu.VMEM((ATT_KV_HEADS, kc // 2, rep * tq), BF16),
            pltpu.VMEM((ATT_KV_HEADS, kc // 2, rep * tq), BF16),
        ],
        compiler_params=pltpu.CompilerParams(
            dimension_semantics=("arbitrary", "arbitrary"), vmem_limit_bytes=VMEM_LIMIT),
        name="dsa",
    )(q, qi, wi_t, k, v_t, ki, h, w_out.astype(BF16), upper)


def kernel(x, positions, ffn1_norm, ffn1_wi, ffn1_wo, mix_norm, ffn2_norm, ffn2_wi, ffn2_wo, even_w_in, gla_gate_w, gla_gate_b, gla_out_norm, pool_w, pool_scale, even_w_out, odd_w_in, odd_w_out, final_norm):
    b, s, d = x.shape
    depth = ffn1_wi.shape[0]
    h = x
    for li in range(depth):
        h = _ffn(h.reshape(b * s, d), ffn1_norm[li], ffn1_wi, ffn1_wo, li, final_norm, final=False)
        h = h.reshape(b, s, d)
        j = li // 2
        if li % 2 == 0:
            h = _even_mixer(h, mix_norm[li], even_w_in[j], gla_gate_w[j], gla_gate_b[j], gla_out_norm[j],
                            pool_w[j], pool_scale[j], even_w_out[j])
        else:
            q, k, v_t, qi, ki, wi_t = _odd_proj(h, mix_norm[li], odd_w_in[j], positions)
            h = _dsa(h, q, k, v_t, qi, ki, wi_t, odd_w_out[j])
        last = li == depth - 1
        h = _ffn(h.reshape(b * s, d), ffn2_norm[li], ffn2_wi, ffn2_wo, li, final_norm, final=last)
        h = h.reshape(b, s, d)
    return h
```

```python
import functools

import jax
import jax.numpy as jnp
import numpy as np
from jax import lax
from jax.experimental import pallas as pl
from jax.experimental.pallas import tpu as pltpu

F32 = jnp.float32
BF16 = jnp.bfloat16

RMS_EPS = 1e-6
ROPE_THETA = 10000.0

GLA_HEADS = 4
GLA_DK = 64
GLA_DV = 128
GLA_GATE_RANK = 16
GLA_GATE_TAU = 16.0

POOL_WINDOWS = (2, 4, 8, 16)
POOL_GROUP_DIM = 128
POOL_HALO = 16

ATT_HEADS = 8
ATT_KV_HEADS = 2
ATT_HEAD_DIM = 128
IDX_HEADS = 4
IDX_DIM = 64
TOPK_MAX = 256
DSA_QUERY_TILE = 256

LANES = 128
SUBLANES = 8
BF16_SUBLANES = 16
FFN_TOKEN_TILE = 512
EVEN_TOKEN_TILE = 256
MXU_WIDTH = 256
MASKED_SCORE = -1e30
SCAN_ROWS = 128
COUNT_ROWS = 32
BISECT_STEPS = 20
DSA_KEY_CHUNK = 512
LOG2_E = 1.4426950408889634
VMEM_LIMIT = 56 * 1024 * 1024


def _rms(x, g):
    ms = jnp.mean(x * x, axis=-1, keepdims=True)
    return x * lax.rsqrt(ms + RMS_EPS) * g


def _dot(a, b):
    return jnp.dot(a, b, preferred_element_type=F32)


def _dot_nt(a, b):
    return lax.dot_general(a, b, (((1,), (1,)), ((), ())), preferred_element_type=F32)


def _dot_tn(a, b):
    return lax.dot_general(a, b, (((0,), (0,)), ((), ())), preferred_element_type=F32)


def _const_spec(shape):
    n = len(shape)
    return pl.BlockSpec(shape, lambda *_: (0,) * n)


def _ffn_kernel(h_ref, g_ref, wi_ref, wo_ref, fg_ref, o_ref, xn_ref, acc_ref, *, f, tf, final):
    x = h_ref[...]
    xn_ref[...] = _rms(x, g_ref[...]).astype(BF16)
    for c in range(f // tf):
        xn = xn_ref[...]
        gate = _dot(xn, wi_ref[0, :, c * tf:(c + 1) * tf].astype(BF16))
        up = _dot(xn, wi_ref[0, :, f + c * tf:f + (c + 1) * tf].astype(BF16))
        act = gate * (1.0 / (1.0 + jnp.exp(-gate))) * up
        y = _dot(act.astype(BF16), wo_ref[0, c * tf:(c + 1) * tf, :].astype(BF16))
        if c == 0:
            acc_ref[...] = y
        else:
            acc_ref[...] += y
    out = x + 0.5 * acc_ref[...]
    if final:
        out = _rms(out, fg_ref[...])
    o_ref[...] = out


def _ffn(h, norm_g, wi_all, wo_all, layer, final_g, *, final, tm=FFN_TOKEN_TILE, tf=MXU_WIDTH):
    n, d = h.shape
    f = wo_all.shape[1]
    assert f % tf == 0 and n % tm == 0
    return pl.pallas_call(
        functools.partial(_ffn_kernel, f=f, tf=tf, final=final),
        grid=(n // tm,),
        in_specs=[
            pl.BlockSpec((tm, d), lambda i: (i, 0)),
            _const_spec((1, d)),
            pl.BlockSpec((1, d, 2 * f), lambda i: (layer, 0, 0), pipeline_mode=pl.Buffered(1)),
            pl.BlockSpec((1, f, d), lambda i: (layer, 0, 0), pipeline_mode=pl.Buffered(1)),
            _const_spec((1, d)),
        ],
        out_specs=pl.BlockSpec((tm, d), lambda i: (i, 0)),
        out_shape=jax.ShapeDtypeStruct((n, d), F32),
        scratch_shapes=[pltpu.VMEM((tm, d), BF16), pltpu.VMEM((tm, d), F32)],
        compiler_params=pltpu.CompilerParams(
            dimension_semantics=("arbitrary",), vmem_limit_bytes=VMEM_LIMIT),
        name="ffn_final" if final else "ffn",
    )(h, norm_g.reshape(1, d), wi_all, wo_all, final_g.reshape(1, d))


def _log_sigmoid(z):
    return jnp.minimum(z, 0.0) - jnp.log1p(jnp.exp(-jnp.abs(z)))


def _split_bf16(x, n):
    pieces = []
    for _ in range(n - 1):
        hi = x.astype(BF16)
        pieces.append(hi)
        x = x - hi.astype(F32)
    pieces.append(x.astype(BF16))
    return pieces


def _dot_pieces(sel, x, n):
    pieces = _split_bf16(x, n)
    out = _dot(sel, pieces[0])
    for piece in pieces[1:]:
        out = out + _dot(sel, piece)
    return out


def _gla_levels(tt):
    return [tt >> i for i in range(tt.bit_length() - 1)]


def _gla_constants(tt):
    levels = _gla_levels(tt)
    row = np.arange(tt)[:, None]
    col = np.arange(tt)[None, :]
    tril = (col <= row).astype(np.float32)
    gsel = np.concatenate([(col == (row // s) * s + s // 2 - 1).astype(np.float32) for s in levels], axis=0)
    lmask = np.stack([((row // s == col // s) & (row % s >= s // 2) & (col % s < s // 2)).astype(np.float32)
                      for s in levels])
    return jnp.asarray(tril, BF16), jnp.asarray(gsel, BF16), jnp.asarray(lmask, F32)


def _even_kernel(h_ref, g_ref, wm_ref, wa_ref, gw_ref, gb_ref, on_ref, pw_ref, ps_ref, wo_ref,
                 tril_ref, gsel_ref, lmask_ref, o_ref, state_ref, ext_ref, mix_ref, *, tt):
    t_idx = pl.program_id(1)
    hk = GLA_HEADS * GLA_DK
    hv = GLA_HEADS * GLA_DV

    @pl.when(t_idx == 0)
    def _():
        state_ref[...] = jnp.zeros_like(state_ref)
        ext_ref[0:POOL_HALO, :] = jnp.zeros((POOL_HALO, ext_ref.shape[1]), F32)

    x = h_ref[0]
    hn = _rms(x, g_ref[...]).astype(BF16)
    a_lr = _dot(hn, wa_ref[...])
    a_hi, a_lo = _split_bf16(a_lr, 2)
    z = _dot(a_hi, gw_ref[0]) + (_dot(a_hi, gw_ref[1]) + _dot(a_lo, gw_ref[0])) + gb_ref[...]
    la = _log_sigmoid(z) * (1.0 / GLA_GATE_TAU)
    proj_u = _dot(hn, wm_ref[:, 2 * hk + 2 * hv:])

    bc = _dot_pieces(tril_ref[...], la, 3)
    proj_v = _dot(hn, wm_ref[:, 2 * hk:2 * hk + hv])

    levels = _gla_levels(tt)
    ref_all = _dot_pieces(gsel_ref[...], bc, 2)

    ext_ref[POOL_HALO:POOL_HALO + tt, :] = proj_u
    tpos = (t_idx * tt + lax.broadcasted_iota(jnp.int32, (tt, 1), 0) + 1).astype(F32)
    for gi, w in enumerate(POOL_WINDOWS):
        ls = slice(gi * POOL_GROUP_DIM, (gi + 1) * POOL_GROUP_DIM)
        ug = ext_ref[POOL_HALO:POOL_HALO + tt, ls]
        win = ug
        for dlt in range(1, w):
            win = win + ext_ref[POOL_HALO - dlt:POOL_HALO - dlt + tt, ls]
        pg = win / jnp.minimum(tpos, float(w)) - ug
        pg = _dot(pg.astype(BF16), pw_ref[gi]) * ps_ref[:, ls]
        mix_ref[:, hv + gi * POOL_GROUP_DIM: hv + (gi + 1) * POOL_GROUP_DIM] = pg.astype(BF16)
    ext_ref[0:POOL_HALO, :] = ext_ref[tt:tt + POOL_HALO, :]

    proj_qk = _dot(hn, wm_ref[:, 0:2 * hk])
    proj_g = _dot(hn, wm_ref[:, 2 * hk + hv:2 * hk + 2 * hv])
    rowk = lax.broadcasted_iota(jnp.int32, (tt, hk), 0)
    q_scaled = []
    k_scaled = []
    for li, s in enumerate(levels):
        ref_b = ref_all[li * tt:(li + 1) * tt]
        upper = (rowk & (s - 1)) >= s // 2
        q_scaled.append(jnp.exp(jnp.where(upper, bc - ref_b, 0.0)))
        k_scaled.append(jnp.exp(jnp.where(upper, 0.0, ref_b - bc)))

    e_b = jnp.exp(bc)
    b_last = bc[tt - 1:tt, :]
    e_last = jnp.exp(b_last)
    e_dec = jnp.exp(b_last - bc)

    for hd in range(GLA_HEADS):
        ks = slice(hd * GLA_DK, (hd + 1) * GLA_DK)
        qh = proj_qk[:, ks] * (GLA_DK ** -0.5)
        kh = proj_qk[:, hk + hd * GLA_DK: hk + (hd + 1) * GLA_DK]
        vh = proj_v[:, hd * GLA_DV:(hd + 1) * GLA_DV]
        vh_b = vh.astype(BF16)
        scores = jnp.zeros((tt, tt), F32)
        for li in range(len(levels)):
            qs = (qh * q_scaled[li][:, ks]).astype(BF16)
            kss = (kh * k_scaled[li][:, ks]).astype(BF16)
            scores = scores + jnp.where(lmask_ref[li] > 0.0, _dot_nt(qs, kss), 0.0)
        diag = jnp.sum(qh * kh, axis=-1, keepdims=True)
        st = state_ref[hd]
        inter = _dot_nt((qh * e_b[:, ks]).astype(BF16), st.astype(BF16))
        o_h = inter + _dot(scores.astype(BF16), vh_b) + diag * vh
        state_ref[hd] = st * e_last[:, ks] + _dot_tn(vh_b, (kh * e_dec[:, ks]).astype(BF16))
        gh = proj_g[:, hd * GLA_DV:(hd + 1) * GLA_DV]
        o_n = _rms(o_h, on_ref[:, hd * GLA_DV:(hd + 1) * GLA_DV])
        mix_ref[:, hd * GLA_DV:(hd + 1) * GLA_DV] = (o_n * gh * (1.0 / (1.0 + jnp.exp(-gh)))).astype(BF16)

    o_ref[0] = x + _dot(mix_ref[...], wo_ref[...])


def _even_mixer(h, norm_g, w_in, gate_w, gate_b, out_norm, pool_w, pool_scale, w_out, *, tt=EVEN_TOKEN_TILE):
    b, s, d = h.shape
    hk = GLA_HEADS * GLA_DK
    hv = GLA_HEADS * GLA_DV
    pool_dim = len(POOL_WINDOWS) * POOL_GROUP_DIM
    o_a = 2 * hk + 2 * hv
    w_main = jnp.concatenate([w_in[:, :o_a], w_in[:, o_a + GLA_GATE_RANK:]], axis=1).astype(BF16)
    w_a = jnp.pad(w_in[:, o_a:o_a + GLA_GATE_RANK], ((0, 0), (0, LANES - GLA_GATE_RANK))).astype(BF16)
    gw = jnp.pad(gate_w, ((0, LANES - GLA_GATE_RANK), (0, 0)))
    gw_hi = gw.astype(BF16)
    gw = jnp.stack([gw_hi, (gw - gw_hi.astype(F32)).astype(BF16)])
    tril, gsel, lmask = _gla_constants(tt)
    n_levels = lmask.shape[0]
    n_main = w_main.shape[1]
    mix_dim = hv + pool_dim
    return pl.pallas_call(
        functools.partial(_even_kernel, tt=tt),
        grid=(b, s // tt),
        in_specs=[
            pl.BlockSpec((1, tt, d), lambda bi, ti: (bi, ti, 0)),
            _const_spec((1, d)),
            _const_spec((d, n_main)),
            _const_spec((d, LANES)),
            _const_spec((2, LANES, hk)),
            _const_spec((1, hk)),
            _const_spec((1, hv)),
            _const_spec((len(POOL_WINDOWS), POOL_GROUP_DIM, POOL_GROUP_DIM)),
            _const_spec((1, pool_dim)),
            _const_spec((mix_dim, d)),
            _const_spec((tt, tt)),
            _const_spec((n_levels * tt, tt)),
            _const_spec((n_levels, tt, tt)),
        ],
        out_specs=pl.BlockSpec((1, tt, d), lambda bi, ti: (bi, ti, 0)),
        out_shape=jax.ShapeDtypeStruct((b, s, d), F32),
        scratch_shapes=[
            pltpu.VMEM((GLA_HEADS, GLA_DV, GLA_DK), F32),
            pltpu.VMEM((POOL_HALO + tt, pool_dim), F32),
            pltpu.VMEM((tt, mix_dim), BF16),
        ],
        compiler_params=pltpu.CompilerParams(
            dimension_semantics=("arbitrary", "arbitrary"), vmem_limit_bytes=VMEM_LIMIT),
        name="even_mixer",
    )(h, norm_g.reshape(1, d), w_main, w_a, gw, gate_b.reshape(1, hk), out_norm.reshape(1, hv),
      pool_w.astype(BF16), pool_scale.reshape(1, pool_dim), w_out.astype(BF16), tril, gsel, lmask)


def _odd_proj_kernel(h_ref, g_ref, w_ref, pos_ref, inv_ref, q_ref, k_ref, vt_ref, qi_ref, ki_ref, wi_ref, *, tm):
    nq = ATT_HEADS * ATT_HEAD_DIM
    nkv = ATT_KV_HEADS * ATT_HEAD_DIM
    ni = IDX_HEADS * IDX_DIM
    hn = _rms(h_ref[0], g_ref[...]).astype(BF16)

    def proj_cols(lo, width):
        return _dot(hn, w_ref[:, lo:lo + width])

    per_dot = MXU_WIDTH // ATT_HEAD_DIM
    o_i = nq + 2 * nkv
    xv = proj_cols(nq + nkv, nkv)
    for gi in range(ATT_KV_HEADS):
        vt_ref[0, gi, 0] = xv[:, gi * ATT_HEAD_DIM:(gi + 1) * ATT_HEAD_DIM].T.astype(BF16)
    xqs = [proj_cols(h0 * ATT_HEAD_DIM, MXU_WIDTH) for h0 in range(0, ATT_HEADS, per_dot)]
    xk = proj_cols(nq, nkv)
    xi = proj_cols(o_i, ni)
    tail = proj_cols(o_i + ni, LANES)
    wi_ref[0] = pltpu.roll(tail, LANES - IDX_DIM, 1).T[0:SUBLANES, :]

    pos = pos_ref[0]
    lane = lax.broadcasted_iota(jnp.int32, (tm, LANES), 1)

    half_a = ATT_HEAD_DIM // 2
    half_i = IDX_DIM // 2
    ang = pos * inv_ref[0:1, :]
    cos_p = jnp.cos(ang)
    sin_p = jnp.sin(ang)

    def att_table(t):
        return jnp.where(lane < half_a, t, pltpu.roll(t, half_a, 1))

    def idx_table(t):
        t32 = jnp.where((lane & (2 * half_i - 1)) < half_i, pltpu.roll(t, half_a, 1), pltpu.roll(t, half_a + half_i, 1))
        return jnp.where(lane < IDX_DIM, t32, pltpu.roll(t32, IDX_DIM, 1))

    cos_a = att_table(cos_p)
    sin_a = att_table(sin_p)
    sin_a = jnp.where(lane < half_a, -sin_a, sin_a)

    def rope_att(xh):
        return xh * cos_a + pltpu.roll(xh, ATT_HEAD_DIM // 2, 1) * sin_a

    cos_i = idx_table(cos_p)
    sin_i = idx_table(sin_p)
    low_half = (lane % IDX_DIM) < IDX_DIM // 2

    def rope_idx(xb):
        rot = jnp.where(low_half, -pltpu.roll(xb, LANES - IDX_DIM // 2, 1), pltpu.roll(xb, IDX_DIM // 2, 1))
        return xb * cos_i + rot * sin_i

    for g, xq in enumerate(xqs):
        for hd in range(per_dot):
            xh = xq[:, hd * ATT_HEAD_DIM:(hd + 1) * ATT_HEAD_DIM]
            q_ref[0, g * per_dot + hd] = (rope_att(xh) * (LOG2_E * ATT_HEAD_DIM ** -0.5)).astype(BF16)
    for gi in range(ATT_KV_HEADS):
        xh = xk[:, gi * ATT_HEAD_DIM:(gi + 1) * ATT_HEAD_DIM]
        k_ref[0, :, gi * ATT_HEAD_DIM:(gi + 1) * ATT_HEAD_DIM] = rope_att(xh).astype(BF16)
    for pi in range(ni // LANES):
        xr = rope_idx(xi[:, pi * LANES:(pi + 1) * LANES]).astype(BF16)
        for sub in range(LANES // IDX_DIM):
            qi_ref[0, pi * (LANES // IDX_DIM) + sub] = xr[:, sub * IDX_DIM:(sub + 1) * IDX_DIM]
    ki_ref[0] = rope_idx(tail)[:, :IDX_DIM].astype(BF16)


def _odd_proj(h, norm_g, w_in, positions):
    b, s, d = h.shape
    tm = _dsa_key_chunk(s)
    n_in = w_in.shape[1]
    n_pad = -(-n_in // LANES) * LANES
    w = jnp.pad(w_in, ((0, 0), (0, n_pad - n_in))).astype(BF16)
    inv_a = ROPE_THETA ** (-jnp.arange(0, ATT_HEAD_DIM, 2, dtype=F32) / ATT_HEAD_DIM)
    inv_i = ROPE_THETA ** (-jnp.arange(0, IDX_DIM, 2, dtype=F32) / IDX_DIM)
    inv = jnp.concatenate([inv_a, inv_i])
    inv = jnp.pad(inv[None, :], ((0, SUBLANES - 1), (0, LANES - inv.shape[0])))
    pos = positions.astype(F32).reshape(b, s, 1)
    return pl.pallas_call(
        functools.partial(_odd_proj_kernel, tm=tm),
        grid=(b, s // tm),
        in_specs=[
            pl.BlockSpec((1, tm, d), lambda bi, ti: (bi, ti, 0)),
            _const_spec((1, d)),
            _const_spec((d, n_pad)),
            pl.BlockSpec((1, tm, 1), lambda bi, ti: (bi, ti, 0)),
            _const_spec((SUBLANES, LANES)),
        ],
        out_specs=[
            pl.BlockSpec((1, ATT_HEADS, tm, ATT_HEAD_DIM), lambda bi, ti: (bi, 0, ti, 0)),
            pl.BlockSpec((1, tm, ATT_KV_HEADS * ATT_HEAD_DIM), lambda bi, ti: (bi, ti, 0)),
            pl.BlockSpec((1, ATT_KV_HEADS, 1, ATT_HEAD_DIM, tm), lambda bi, ti: (bi, 0, ti, 0, 0)),
            pl.BlockSpec((1, IDX_HEADS, tm, IDX_DIM), lambda bi, ti: (bi, 0, ti, 0)),
            pl.BlockSpec((1, tm, IDX_DIM), lambda bi, ti: (bi, ti, 0)),
            pl.BlockSpec((1, SUBLANES, tm), lambda bi, ti: (bi, 0, ti)),
        ],
        out_shape=[
            jax.ShapeDtypeStruct((b, ATT_HEADS, s, ATT_HEAD_DIM), BF16),
            jax.ShapeDtypeStruct((b, s, ATT_KV_HEADS * ATT_HEAD_DIM), BF16),
            jax.ShapeDtypeStruct((b, ATT_KV_HEADS, s // tm, ATT_HEAD_DIM, tm), BF16),
            jax.ShapeDtypeStruct((b, IDX_HEADS, s, IDX_DIM), BF16),
            jax.ShapeDtypeStruct((b, s, IDX_DIM), BF16),
            jax.ShapeDtypeStruct((b, SUBLANES, s), F32),
        ],
        compiler_params=pltpu.CompilerParams(
            dimension_semantics=("arbitrary", "arbitrary"), vmem_limit_bytes=VMEM_LIMIT),
        name="odd_proj",
    )(h, norm_g.reshape(1, d), w, pos, inv)


def _float_to_ordered(x):
    bits = lax.bitcast_convert_type(x, jnp.int32)
    return jnp.where(bits >= 0, bits, bits ^ jnp.int32(0x7FFFFFFF))


def _ordered_to_float(o):
    bits = jnp.where(o >= 0, o, o ^ jnp.int32(0x7FFFFFFF))
    return lax.bitcast_convert_type(bits, F32)


def _dsa_kernel(q_ref, qi_ref, wi_ref, k_ref, vt_ref, ki_ref, h_ref, wo_ref, upper_ref, o_ref,
                sc_ref, bias_ref, qe_ref, acc_ref, og_ref, sta_ref, stb_ref, pa_ref, pb_ref, *, tq, kc, top_k):
    qb = pl.program_id(1)
    n_kc = (qb * tq + tq + kc - 1) // kc
    rep = ATT_HEADS // ATT_KV_HEADS
    q_pos = qb * tq + lax.broadcasted_iota(jnp.int32, (1, tq), 1)
    scan_iota = lax.broadcasted_iota(jnp.int32, (SCAN_ROWS, tq), 0)

    qi = qi_ref[0].reshape(IDX_HEADS * tq, IDX_DIM)
    wi = wi_ref[0] * ((IDX_DIM ** -0.5) * (IDX_HEADS ** -0.5))

    def fold(vals, op):
        return op(vals.reshape(vals.shape[0] // COUNT_ROWS, COUNT_ROWS, tq), axis=0)

    def score_body(c, carry):
        mx_p, mn_p = carry
        for r0 in range(0, kc, SCAN_ROWS):
            off = pl.multiple_of(c * kc + r0, SCAN_ROWS)
            lg = jnp.maximum(_dot_nt(ki_ref[0, pl.ds(off, SCAN_ROWS), :], qi), 0.0)
            isc = lg[:, 0:tq] * wi[0:1, :]
            for hd in range(1, IDX_HEADS):
                isc = isc + lg[:, hd * tq:(hd + 1) * tq] * wi[hd:hd + 1, :]
            sc = jnp.where(scan_iota + off <= q_pos, isc, -jnp.inf)
            sc_ref[c, r0:r0 + SCAN_ROWS, :] = sc
            mx_p = jnp.maximum(mx_p, fold(sc, jnp.max))
            mn_p = jnp.minimum(mn_p, fold(isc, jnp.min))
        return mx_p, mn_p

    mx_p, mn_p = lax.fori_loop(0, n_kc, score_body, (jnp.full((COUNT_ROWS, tq), -jnp.inf, F32),
                                                     jnp.full((COUNT_ROWS, tq), jnp.inf, F32)))
    mx = jnp.max(mx_p, axis=0, keepdims=True) + 0.0
    mn = jnp.min(mn_p, axis=0, keepdims=True)

    def count_ge(t):
        def body(c, acc):
            for r0 in range(0, kc, SCAN_ROWS):
                xs = sc_ref[c, r0:r0 + SCAN_ROWS, :].reshape(SCAN_ROWS // COUNT_ROWS, COUNT_ROWS, tq)
                acc = acc + jnp.sum(jnp.where(xs >= t, 1.0, 0.0), axis=0)
            return acc
        part = lax.fori_loop(0, n_kc, body, jnp.zeros((COUNT_ROWS, tq), F32))
        return jnp.sum(part, axis=0, keepdims=True)

    def max_below(t):
        def body(c, acc):
            for r0 in range(0, kc, SCAN_ROWS):
                xs = sc_ref[c, r0:r0 + SCAN_ROWS, :]
                acc = jnp.maximum(acc, fold(jnp.where(xs < t, xs, -jnp.inf), jnp.max))
            return acc
        part = lax.fori_loop(0, n_kc, body, jnp.full((COUNT_ROWS, tq), -jnp.inf, F32))
        return jnp.max(part, axis=0, keepdims=True)

    kf = float(top_k)
    n_adm = (q_pos + 1).astype(F32)
    small = n_adm <= kf
    state = (jnp.where(small, 0.0, mn),
             jnp.where(small, 1.0, _ordered_to_float(_float_to_ordered(mx) + 1)),
             n_adm,
             jnp.where(small, 1.0, 0.0),
             jnp.full((1, tq), -jnp.inf, F32),
             jnp.full((1, tq), kf, F32))

    def bis_body(_, st):
        lo, hi, c_lo, done, thr, c_thr = st
        mid = 0.5 * lo + 0.5 * hi
        c = count_ge(mid)
        active = done == 0.0
        stuck = (mid <= lo) | (mid >= hi)
        fin_stuck = active & stuck
        fin_hit = active & (~stuck) & (c == kf)
        up = active & (~stuck) & (c >= kf)
        down = active & (~stuck) & (c < kf)
        thr = jnp.where(fin_stuck, lo, jnp.where(fin_hit, mid, thr))
        c_thr = jnp.where(fin_stuck, c_lo, c_thr)
        done = jnp.where(fin_stuck | fin_hit, 1.0, done)
        return (jnp.where(up, mid, lo), jnp.where(down, mid, hi), jnp.where(up, c, c_lo), done, thr, c_thr)

    _, hi, _, done, thr, c_thr = lax.fori_loop(0, BISECT_STEPS, bis_body, state)

    def peel_body(st):
        _, hi, done, thr, c_thr = st
        v = max_below(hi)
        c = count_ge(v)
        active = done == 0.0
        ok = active & (c >= kf)
        done = jnp.where(ok, 1.0, done)
        return (jnp.sum(1.0 - done), jnp.where(active & (c < kf), v, hi), done,
                jnp.where(ok, v, thr), jnp.where(ok, c, c_thr))

    _, _, _, thr, c_thr = lax.while_loop(lambda st: st[0] > 0.0, peel_body,
                                         (jnp.sum(1.0 - done), hi, done, thr, c_thr))

    ka = kc // 2
    excess = c_thr - kf
    half_iota = lax.broadcasted_iota(jnp.int32, (ka, tq), 0)

    def mask_body(j, later_total):
        c = n_kc - 1 - j
        halves = []
        for half in (1, 0):
            xs = sc_ref[c, half * ka:(half + 1) * ka, :]
            ties = jnp.where(xs == thr, 1.0, 0.0)
            halves.append((half, xs, ties, _dot(upper_ref[...], ties.astype(BF16))))
        for half, xs, ties, later_in_half in halves:
            later = later_in_half + later_total
            sel = jnp.where(xs > thr, 0.0,
                            jnp.where(ties > 0.0, jnp.where(later >= excess, 0.0, MASKED_SCORE), MASKED_SCORE))
            kidx = half_iota + (c * kc + half * ka)
            bias_ref[c, half * ka:(half + 1) * ka, :] = jnp.where(kidx <= q_pos, sel, MASKED_SCORE).astype(BF16)
            later_total = later[0:1, :] + ties[0:1, :]
        return later_total

    lax.fori_loop(0, n_kc, mask_body, jnp.zeros((1, tq), F32))

    acc_ref[...] = jnp.zeros_like(acc_ref)
    ones_rows = jnp.ones((BF16_SUBLANES, ka), BF16)
    n_qs = tq // LANES
    own_col = ((lax.broadcasted_iota(jnp.int32, (rep * LANES, LANES), 0) & (LANES - 1))
               == lax.broadcasted_iota(jnp.int32, (rep * LANES, LANES), 1))
    for gi in range(ATT_KV_HEADS):
        for sb in range(n_qs):
            qs = q_ref[0, gi * rep:(gi + 1) * rep, sb * LANES:(sb + 1) * LANES, :]
            qe_ref[gi, sb, :, 0:ATT_HEAD_DIM] = qs.reshape(rep * LANES, ATT_HEAD_DIM)
            qe_ref[gi, sb, :, ATT_HEAD_DIM:] = jnp.where(own_col, 1.0, 0.0).astype(BF16)

    def scores_into(dst_ref, c, half):
        off = pl.multiple_of(c * kc + half * ka, ka)
        bias = bias_ref[c, half * ka:(half + 1) * ka, :]
        tops = []
        for gi in range(ATT_KV_HEADS):
            kb = k_ref[0, pl.ds(off, ka), gi * ATT_HEAD_DIM:(gi + 1) * ATT_HEAD_DIM]
            st = jnp.concatenate(
                [_dot_nt(jnp.concatenate([kb, bias[:, sb * LANES:(sb + 1) * LANES]], axis=1), qe_ref[gi, sb])
                 for sb in range(n_qs)], axis=1)
            dst_ref[gi] = st
            tops.append(jnp.max(st, axis=0, keepdims=True))
        return tuple(tops)

    def softmax_into(p_ref, src_ref, m_all, tops):
        m_out, alphas = [], []
        for gi in range(ATT_KV_HEADS):
            m_new = jnp.maximum(m_all[gi], tops[gi])
            alphas.append(jnp.exp2(m_all[gi] - m_new))
            p_ref[gi] = jnp.exp2(src_ref[gi] - m_new).astype(BF16)
            m_out.append(m_new)
        return tuple(m_out), tuple(alphas)

    def values_in(p_ref, c, half, alphas):
        for gi in range(ATT_KV_HEADS):
            vt = jnp.concatenate([vt_ref[0, gi, c, :, half * ka:(half + 1) * ka], ones_rows], axis=0)
            acc_ref[gi] = alphas[gi] * acc_ref[gi] + _dot(vt, p_ref[gi])

    def att_body(c, carry):
        m_all, tops_a, alpha_b = carry
        tops_b = scores_into(stb_ref, c, 1)
        values_in(pb_ref, jnp.maximum(c - 1, 0), 1, alpha_b)
        m_all, alpha_a = softmax_into(pa_ref, sta_ref, m_all, tops_a)
        tops_a = scores_into(sta_ref, jnp.minimum(c + 1, n_kc - 1), 0)
        values_in(pa_ref, c, 0, alpha_a)
        m_all, alpha_b = softmax_into(pb_ref, stb_ref, m_all, tops_b)
        return m_all, tops_a, alpha_b

    pb_ref[...] = jnp.zeros_like(pb_ref)
    m0 = (jnp.full((1, rep * tq), MASKED_SCORE, F32),) * ATT_KV_HEADS
    one = (jnp.ones((1, rep * tq), F32),) * ATT_KV_HEADS
    _, _, alpha_b = lax.fori_loop(0, n_kc, att_body, (m0, scores_into(sta_ref, 0, 0), one))
    values_in(pb_ref, n_kc - 1, 1, alpha_b)
    for gi in range(ATT_KV_HEADS):
        acc = acc_ref[gi]
        og = (acc[0:ATT_HEAD_DIM] / acc[ATT_HEAD_DIM:ATT_HEAD_DIM + 1]).astype(BF16)
        for sb in range(n_qs):
            for r in range(rep):
                hd = gi * rep + r
                col = (sb * rep + r) * LANES
                og_ref[hd * ATT_HEAD_DIM:(hd + 1) * ATT_HEAD_DIM, sb * LANES:(sb + 1) * LANES] = og[:, col:col + LANES]

    o_ref[0] = h_ref[0] + _dot_tn(og_ref[...], wo_ref[...])


def _dsa_key_chunk(s):
    return min(DSA_KEY_CHUNK, s)


def _dsa(h, q, k, v_t, qi, ki, wi_t, w_out):
    b, s, d = h.shape
    tq = DSA_QUERY_TILE
    top_k = min(TOPK_MAX, s // 4)
    kc = _dsa_key_chunk(s)
    assert kc >= top_k and s % kc == 0 and kc % (2 * tq) == 0
    rep = ATT_HEADS // ATT_KV_HEADS
    nkv = ATT_KV_HEADS * ATT_HEAD_DIM
    n_chunks = s // kc
    vrows = ATT_HEAD_DIM + BF16_SUBLANES
    ka = kc // 2
    upper = jnp.asarray(np.arange(ka)[None, :] > np.arange(ka)[:, None], BF16)
    return pl.pallas_call(
        functools.partial(_dsa_kernel, tq=tq, kc=kc, top_k=top_k),
        grid=(b, s // tq),
        in_specs=[
            pl.BlockSpec((1, ATT_HEADS, tq, ATT_HEAD_DIM), lambda bi, qi_: (bi, 0, qi_, 0)),
            pl.BlockSpec((1, IDX_HEADS, tq, IDX_DIM), lambda bi, qi_: (bi, 0, qi_, 0)),
            pl.BlockSpec((1, SUBLANES, tq), lambda bi, qi_: (bi, 0, qi_)),
            pl.BlockSpec((1, s, nkv), lambda bi, qi_: (bi, 0, 0)),
            pl.BlockSpec((1, ATT_KV_HEADS, n_chunks, ATT_HEAD_DIM, kc), lambda bi, qi_: (bi, 0, 0, 0, 0)),
            pl.BlockSpec((1, s, IDX_DIM), lambda bi, qi_: (bi, 0, 0)),
            pl.BlockSpec((1, tq, d), lambda bi, qi_: (bi, qi_, 0)),
            _const_spec((ATT_HEADS * ATT_HEAD_DIM, d)),
            _const_spec((ka, ka)),
        ],
        out_specs=pl.BlockSpec((1, tq, d), lambda bi, qi_: (bi, qi_, 0)),
        out_shape=jax.ShapeDtypeStruct((b, s, d), F32),
        scratch_shapes=[
            pltpu.VMEM((n_chunks, kc, tq), F32),
            pltpu.VMEM((n_chunks, kc, tq), BF16),
            pltpu.VMEM((ATT_KV_HEADS, tq // LANES, rep * LANES, 2 * ATT_HEAD_DIM), BF16),
            pltpu.VMEM((ATT_KV_HEADS, vrows, rep * tq), F32),
            pltpu.VMEM((ATT_HEADS * ATT_HEAD_DIM, tq), BF16),
            pltpu.VMEM((ATT_KV_HEADS, kc // 2, rep * tq), F32),
            pltpu.VMEM((ATT_KV_HEADS, kc // 2, rep * tq), F32),
            pltpu.VMEM((ATT_KV_HEADS, kc // 2, rep * tq), BF16),
            pltpu.VMEM((ATT_KV_HEADS, kc // 2, rep * tq), BF16),
        ],
        compiler_params=pltpu.CompilerParams(
            dimension_semantics=("arbitrary", "arbitrary"), vmem_limit_bytes=VMEM_LIMIT),
        name="dsa",
    )(q, qi, wi_t, k, v_t, ki, h, w_out.astype(BF16), upper)


def kernel(x, positions, ffn1_norm, ffn1_wi, ffn1_wo, mix_norm, ffn2_norm, ffn2_wi, ffn2_wo, even_w_in, gla_gate_w, gla_gate_b, gla_out_norm, pool_w, pool_scale, even_w_out, odd_w_in, odd_w_out, final_norm):
    b, s, d = x.shape
    depth = ffn1_wi.shape[0]
    h = x
    for li in range(depth):
        h = _ffn(h.reshape(b * s, d), ffn1_norm[li], ffn1_wi, ffn1_wo, li, final_norm, final=False)
        h = h.reshape(b, s, d)
        j = li // 2
        if li % 2 == 0:
            h = _even_mixer(h, mix_norm[li], even_w_in[j], gla_gate_w[j], gla_gate_b[j], gla_out_norm[j],
                            pool_w[j], pool_scale[j], even_w_out[j])
        else:
            q, k, v_t, qi, ki, wi_t = _odd_proj(h, mix_norm[li], odd_w_in[j], positions)
            h = _dsa(h, q, k, v_t, qi, ki, wi_t, odd_w_out[j])
        last = li == depth - 1
        h = _ffn(h.reshape(b * s, d), ffn2_norm[li], ffn2_wi, ffn2_wo, li, final_norm, final=last)
        h = h.reshape(b, s, d)
    return h
```

```python
import functools

import jax
import jax.numpy as jnp
import numpy as np
from jax import lax
from jax.experimental import pallas as pl
from jax.experimental.pallas import tpu as pltpu

F32 = jnp.float32
BF16 = jnp.bfloat16

RMS_EPS = 1e-6
ROPE_THETA = 10000.0

GLA_HEADS = 4
GLA_DK = 64
GLA_DV = 128
GLA_GATE_RANK = 16
GLA_GATE_TAU = 16.0

POOL_WINDOWS = (2, 4, 8, 16)
POOL_GROUP_DIM = 128
POOL_HALO = 16

ATT_HEADS = 8
ATT_KV_HEADS = 2
ATT_HEAD_DIM = 128
IDX_HEADS = 4
IDX_DIM = 64
TOPK_MAX = 256
DSA_QUERY_TILE = 256

LANES = 128
SUBLANES = 8
BF16_SUBLANES = 16
FFN_TOKEN_TILE = 512
EVEN_TOKEN_TILE = 256
MXU_WIDTH = 256
MASKED_SCORE = -1e30
SCAN_ROWS = 128
COUNT_ROWS = 32
BISECT_STEPS = 20
DSA_KEY_CHUNK = 512
LOG2_E = 1.4426950408889634
VMEM_LIMIT = 56 * 1024 * 1024


def _rms(x, g):
    ms = jnp.mean(x * x, axis=-1, keepdims=True)
    return x * lax.rsqrt(ms + RMS_EPS) * g


def _dot(a, b):
    return jnp.dot(a, b, preferred_element_type=F32)


def _dot_nt(a, b):
    return lax.dot_general(a, b, (((1,), (1,)), ((), ())), preferred_element_type=F32)


def _dot_tn(a, b):
    return lax.dot_general(a, b, (((0,), (0,)), ((), ())), preferred_element_type=F32)


def _const_spec(shape):
    n = len(shape)
    return pl.BlockSpec(shape, lambda *_: (0,) * n)


def _ffn_kernel(h_ref, g_ref, wi_ref, wo_ref, fg_ref, o_ref, xn_ref, acc_ref, *, f, tf, final):
    x = h_ref[...]
    xn_ref[...] = _rms(x, g_ref[...]).astype(BF16)
    for c in range(f // tf):
        xn = xn_ref[...]
        gate = _dot(xn, wi_ref[0, :, c * tf:(c + 1) * tf].astype(BF16))
        up = _dot(xn, wi_ref[0, :, f + c * tf:f + (c + 1) * tf].astype(BF16))
        act = gate * (1.0 / (1.0 + jnp.exp(-gate))) * up
        y = _dot(act.astype(BF16), wo_ref[0, c * tf:(c + 1) * tf, :].astype(BF16))
        if c == 0:
            acc_ref[...] = y
        else:
            acc_ref[...] += y
    out = x + 0.5 * acc_ref[...]
    if final:
        out = _rms(out, fg_ref[...])
    o_ref[...] = out


def _ffn(h, norm_g, wi_all, wo_all, layer, final_g, *, final, tm=FFN_TOKEN_TILE, tf=MXU_WIDTH):
    n, d = h.shape
    f = wo_all.shape[1]
    assert f % tf == 0 and n % tm == 0
    return pl.pallas_call(
        functools.partial(_ffn_kernel, f=f, tf=tf, final=final),
        grid=(n // tm,),
        in_specs=[
            pl.BlockSpec((tm, d), lambda i: (i, 0)),
            _const_spec((1, d)),
            pl.BlockSpec((1, d, 2 * f), lambda i: (layer, 0, 0), pipeline_mode=pl.Buffered(1)),
            pl.BlockSpec((1, f, d), lambda i: (layer, 0, 0), pipeline_mode=pl.Buffered(1)),
            _const_spec((1, d)),
        ],
        out_specs=pl.BlockSpec((tm, d), lambda i: (i, 0)),
        out_shape=jax.ShapeDtypeStruct((n, d), F32),
        scratch_shapes=[pltpu.VMEM((tm, d), BF16), pltpu.VMEM((tm, d), F32)],
        compiler_params=pltpu.CompilerParams(
            dimension_semantics=("arbitrary",), vmem_limit_bytes=VMEM_LIMIT),
        name="ffn_final" if final else "ffn",
    )(h, norm_g.reshape(1, d), wi_all, wo_all, final_g.reshape(1, d))


def _log_sigmoid(z):
    return jnp.minimum(z, 0.0) - jnp.log1p(jnp.exp(-jnp.abs(z)))


def _split_bf16(x, n):
    pieces = []
    for _ in range(n - 1):
        hi = x.astype(BF16)
        pieces.append(hi)
        x = x - hi.astype(F32)
    pieces.append(x.astype(BF16))
    return pieces


def _dot_pieces(sel, x, n):
    pieces = _split_bf16(x, n)
    out = _dot(sel, pieces[0])
    for piece in pieces[1:]:
        out = out + _dot(sel, piece)
    return out


def _gla_levels(tt):
    return [tt >> i for i in range(tt.bit_length() - 1)]


def _gla_constants(tt):
    levels = _gla_levels(tt)
    row = np.arange(tt)[:, None]
    col = np.arange(tt)[None, :]
    tril = (col <= row).astype(np.float32)
    gsel = np.concatenate([(col == (row // s) * s + s // 2 - 1).astype(np.float32) for s in levels], axis=0)
    lmask = np.stack([((row // s == col // s) & (row % s >= s // 2) & (col % s < s // 2)).astype(np.float32)
                      for s in levels])
    return jnp.asarray(tril, BF16), jnp.asarray(gsel, BF16), jnp.asarray(lmask, F32)


def _even_kernel(h_ref, g_ref, wm_ref, wa_ref, gw_ref, gb_ref, on_ref, pw_ref, ps_ref, wo_ref,
                 tril_ref, gsel_ref, lmask_ref, o_ref, state_ref, ext_ref, mix_ref, scores_ref, *, tt):
    t_idx = pl.program_id(1)
    hk = GLA_HEADS * GLA_DK
    hv = GLA_HEADS * GLA_DV

    @pl.when(t_idx == 0)
    def _():
        state_ref[...] = jnp.zeros_like(state_ref)
        ext_ref[0:POOL_HALO, :] = jnp.zeros((POOL_HALO, ext_ref.shape[1]), F32)

    x = h_ref[0]
    hn = _rms(x, g_ref[...]).astype(BF16)
    a_lr = _dot(hn, wa_ref[...])
    a_hi, a_lo = _split_bf16(a_lr, 2)
    z = _dot(a_hi, gw_ref[0]) + (_dot(a_hi, gw_ref[1]) + _dot(a_lo, gw_ref[0])) + gb_ref[...]
    la = _log_sigmoid(z) * (1.0 / GLA_GATE_TAU)
    proj_u = _dot(hn, wm_ref[:, 2 * hk + 2 * hv:])

    bc = _dot_pieces(tril_ref[...], la, 3)
    proj_v = _dot(hn, wm_ref[:, 2 * hk:2 * hk + hv])

    levels = _gla_levels(tt)
    ref_all = _dot_pieces(gsel_ref[...], bc, 2)

    ext_ref[POOL_HALO:POOL_HALO + tt, :] = proj_u
    tpos = (t_idx * tt + lax.broadcasted_iota(jnp.int32, (tt, 1), 0) + 1).astype(F32)
    for gi, w in enumerate(POOL_WINDOWS):
        ls = slice(gi * POOL_GROUP_DIM, (gi + 1) * POOL_GROUP_DIM)
        ug = ext_ref[POOL_HALO:POOL_HALO + tt, ls]
        win = ug
        for dlt in range(1, w):
            win = win + ext_ref[POOL_HALO - dlt:POOL_HALO - dlt + tt, ls]
        pg = win / jnp.minimum(tpos, float(w)) - ug
        pg = _dot(pg.astype(BF16), pw_ref[gi]) * ps_ref[:, ls]
        mix_ref[:, hv + gi * POOL_GROUP_DIM: hv + (gi + 1) * POOL_GROUP_DIM] = pg.astype(BF16)
    ext_ref[0:POOL_HALO, :] = ext_ref[tt:tt + POOL_HALO, :]

    proj_qk = _dot(hn, wm_ref[:, 0:2 * hk])
    proj_g = _dot(hn, wm_ref[:, 2 * hk + hv:2 * hk + 2 * hv])
    rowk = lax.broadcasted_iota(jnp.int32, (tt, hk), 0)
    q_all = proj_qk[:, 0:hk] * (GLA_DK ** -0.5)
    k_all = proj_qk[:, hk:2 * hk]
    for li, s in enumerate(levels):
        ref_b = ref_all[li * tt:(li + 1) * tt]
        upper = (rowk & (s - 1)) >= s // 2
        qs_all = (q_all * jnp.exp(jnp.where(upper, bc - ref_b, 0.0))).astype(BF16)
        ks_all = (k_all * jnp.exp(jnp.where(upper, 0.0, ref_b - bc))).astype(BF16)
        keep = lmask_ref[li] > 0.0
        for hd in range(GLA_HEADS):
            ks = slice(hd * GLA_DK, (hd + 1) * GLA_DK)
            part = jnp.where(keep, _dot_nt(qs_all[:, ks], ks_all[:, ks]), 0.0)
            if li == 0:
                scores_ref[hd] = part
            else:
                scores_ref[hd] += part

    e_b = jnp.exp(bc)
    b_last = bc[tt - 1:tt, :]
    e_last = jnp.exp(b_last)
    e_dec = jnp.exp(b_last - bc)

    for hd in range(GLA_HEADS):
        ks = slice(hd * GLA_DK, (hd + 1) * GLA_DK)
        qh = proj_qk[:, ks] * (GLA_DK ** -0.5)
        kh = proj_qk[:, hk + hd * GLA_DK: hk + (hd + 1) * GLA_DK]
        vh = proj_v[:, hd * GLA_DV:(hd + 1) * GLA_DV]
        vh_b = vh.astype(BF16)
        scores = scores_ref[hd]
        diag = jnp.sum(qh * kh, axis=-1, keepdims=True)
        st = state_ref[hd]
        inter = _dot_nt((qh * e_b[:, ks]).astype(BF16), st.astype(BF16))
        o_h = inter + _dot(scores.astype(BF16), vh_b) + diag * vh
        state_ref[hd] = st * e_last[:, ks] + _dot_tn(vh_b, (kh * e_dec[:, ks]).astype(BF16))
        gh = proj_g[:, hd * GLA_DV:(hd + 1) * GLA_DV]
        o_n = _rms(o_h, on_ref[:, hd * GLA_DV:(hd + 1) * GLA_DV])
        mix_ref[:, hd * GLA_DV:(hd + 1) * GLA_DV] = (o_n * gh * (1.0 / (1.0 + jnp.exp(-gh)))).astype(BF16)

    o_ref[0] = x + _dot(mix_ref[...], wo_ref[...])


def _even_mixer(h, norm_g, w_in, gate_w, gate_b, out_norm, pool_w, pool_scale, w_out, *, tt=EVEN_TOKEN_TILE):
    b, s, d = h.shape
    hk = GLA_HEADS * GLA_DK
    hv = GLA_HEADS * GLA_DV
    pool_dim = len(POOL_WINDOWS) * POOL_GROUP_DIM
    o_a = 2 * hk + 2 * hv
    w_main = jnp.concatenate([w_in[:, :o_a], w_in[:, o_a + GLA_GATE_RANK:]], axis=1).astype(BF16)
    w_a = jnp.pad(w_in[:, o_a:o_a + GLA_GATE_RANK], ((0, 0), (0, LANES - GLA_GATE_RANK))).astype(BF16)
    gw = jnp.pad(gate_w, ((0, LANES - GLA_GATE_RANK), (0, 0)))
    gw_hi = gw.astype(BF16)
    gw = jnp.stack([gw_hi, (gw - gw_hi.astype(F32)).astype(BF16)])
    tril, gsel, lmask = _gla_constants(tt)
    n_levels = lmask.shape[0]
    n_main = w_main.shape[1]
    mix_dim = hv + pool_dim
    return pl.pallas_call(
        functools.partial(_even_kernel, tt=tt),
        grid=(b, s // tt),
        in_specs=[
            pl.BlockSpec((1, tt, d), lambda bi, ti: (bi, ti, 0)),
            _const_spec((1, d)),
            _const_spec((d, n_main)),
            _const_spec((d, LANES)),
            _const_spec((2, LANES, hk)),
            _const_spec((1, hk)),
            _const_spec((1, hv)),
            _const_spec((len(POOL_WINDOWS), POOL_GROUP_DIM, POOL_GROUP_DIM)),
            _const_spec((1, pool_dim)),
            _const_spec((mix_dim, d)),
            _const_spec((tt, tt)),
            _const_spec((n_levels * tt, tt)),
            _const_spec((n_levels, tt, tt)),
        ],
        out_specs=pl.BlockSpec((1, tt, d), lambda bi, ti: (bi, ti, 0)),
        out_shape=jax.ShapeDtypeStruct((b, s, d), F32),
        scratch_shapes=[
            pltpu.VMEM((GLA_HEADS, GLA_DV, GLA_DK), F32),
            pltpu.VMEM((POOL_HALO + tt, pool_dim), F32),
            pltpu.VMEM((tt, mix_dim), BF16),
            pltpu.VMEM((GLA_HEADS, tt, tt), F32),
        ],
        compiler_params=pltpu.CompilerParams(
            dimension_semantics=("arbitrary", "arbitrary"), vmem_limit_bytes=VMEM_LIMIT),
        name="even_mixer",
    )(h, norm_g.reshape(1, d), w_main, w_a, gw, gate_b.reshape(1, hk), out_norm.reshape(1, hv),
      pool_w.astype(BF16), pool_scale.reshape(1, pool_dim), w_out.astype(BF16), tril, gsel, lmask)


def _odd_proj_kernel(h_ref, g_ref, w_ref, pos_ref, inv_ref, q_ref, k_ref, vt_ref, qi_ref, ki_ref, wi_ref, *, tm):
    nq = ATT_HEADS * ATT_HEAD_DIM
    nkv = ATT_KV_HEADS * ATT_HEAD_DIM
    ni = IDX_HEADS * IDX_DIM
    hn = _rms(h_ref[0], g_ref[...]).astype(BF16)

    def proj_cols(lo, width):
        return _dot(hn, w_ref[:, lo:lo + width])

    per_dot = MXU_WIDTH // ATT_HEAD_DIM
    o_i = nq + 2 * nkv
    xv = proj_cols(nq + nkv, nkv)
    for gi in range(ATT_KV_HEADS):
        vt_ref[0, gi, 0] = xv[:, gi * ATT_HEAD_DIM:(gi + 1) * ATT_HEAD_DIM].T.astype(BF16)
    xqs = [proj_cols(h0 * ATT_HEAD_DIM, MXU_WIDTH) for h0 in range(0, ATT_HEADS, per_dot)]
    xk = proj_cols(nq, nkv)
    xi = proj_cols(o_i, ni)
    tail = proj_cols(o_i + ni, LANES)
    wi_ref[0] = pltpu.roll(tail, LANES - IDX_DIM, 1).T[0:SUBLANES, :]

    pos = pos_ref[0]
    lane = lax.broadcasted_iota(jnp.int32, (tm, LANES), 1)

    half_a = ATT_HEAD_DIM // 2
    half_i = IDX_DIM // 2
    ang = pos * inv_ref[0:1, :]
    cos_p = jnp.cos(ang)
    sin_p = jnp.sin(ang)

    def att_table(t):
        return jnp.where(lane < half_a, t, pltpu.roll(t, half_a, 1))

    def idx_table(t):
        t32 = jnp.where((lane & (2 * half_i - 1)) < half_i, pltpu.roll(t, half_a, 1), pltpu.roll(t, half_a + half_i, 1))
        return jnp.where(lane < IDX_DIM, t32, pltpu.roll(t32, IDX_DIM, 1))

    cos_a = att_table(cos_p)
    sin_a = att_table(sin_p)
    sin_a = jnp.where(lane < half_a, -sin_a, sin_a)

    def rope_att(xh):
        return xh * cos_a + pltpu.roll(xh, ATT_HEAD_DIM // 2, 1) * sin_a

    cos_i = idx_table(cos_p)
    sin_i = idx_table(sin_p)
    low_half = (lane % IDX_DIM) < IDX_DIM // 2

    def rope_idx(xb):
        rot = jnp.where(low_half, -pltpu.roll(xb, LANES - IDX_DIM // 2, 1), pltpu.roll(xb, IDX_DIM // 2, 1))
        return xb * cos_i + rot * sin_i

    for g, xq in enumerate(xqs):
        for hd in range(per_dot):
            xh = xq[:, hd * ATT_HEAD_DIM:(hd + 1) * ATT_HEAD_DIM]
            q_ref[0, g * per_dot + hd] = (rope_att(xh) * (LOG2_E * ATT_HEAD_DIM ** -0.5)).astype(BF16)
    for gi in range(ATT_KV_HEADS):
        xh = xk[:, gi * ATT_HEAD_DIM:(gi + 1) * ATT_HEAD_DIM]
        k_ref[0, :, gi * ATT_HEAD_DIM:(gi + 1) * ATT_HEAD_DIM] = rope_att(xh).astype(BF16)
    for pi in range(ni // LANES):
        xr = rope_idx(xi[:, pi * LANES:(pi + 1) * LANES]).astype(BF16)
        for sub in range(LANES // IDX_DIM):
            qi_ref[0, pi * (LANES // IDX_DIM) + sub] = xr[:, sub * IDX_DIM:(sub + 1) * IDX_DIM]
    ki_ref[0] = rope_idx(tail)[:, :IDX_DIM].astype(BF16)


def _odd_proj(h, norm_g, w_in, positions):
    b, s, d = h.shape
    tm = _dsa_key_chunk(s)
    n_in = w_in.shape[1]
    n_pad = -(-n_in // LANES) * LANES
    w = jnp.pad(w_in, ((0, 0), (0, n_pad - n_in))).astype(BF16)
    inv_a = ROPE_THETA ** (-jnp.arange(0, ATT_HEAD_DIM, 2, dtype=F32) / ATT_HEAD_DIM)
    inv_i = ROPE_THETA ** (-jnp.arange(0, IDX_DIM, 2, dtype=F32) / IDX_DIM)
    inv = jnp.concatenate([inv_a, inv_i])
    inv = jnp.pad(inv[None, :], ((0, SUBLANES - 1), (0, LANES - inv.shape[0])))
    pos = positions.astype(F32).reshape(b, s, 1)
    return pl.pallas_call(
        functools.partial(_odd_proj_kernel, tm=tm),
        grid=(b, s // tm),
        in_specs=[
            pl.BlockSpec((1, tm, d), lambda bi, ti: (bi, ti, 0)),
            _const_spec((1, d)),
            _const_spec((d, n_pad)),
            pl.BlockSpec((1, tm, 1), lambda bi, ti: (bi, ti, 0)),
            _const_spec((SUBLANES, LANES)),
        ],
        out_specs=[
            pl.BlockSpec((1, ATT_HEADS, tm, ATT_HEAD_DIM), lambda bi, ti: (bi, 0, ti, 0)),
            pl.BlockSpec((1, tm, ATT_KV_HEADS * ATT_HEAD_DIM), lambda bi, ti: (bi, ti, 0)),
            pl.BlockSpec((1, ATT_KV_HEADS, 1, ATT_HEAD_DIM, tm), lambda bi, ti: (bi, 0, ti, 0, 0)),
            pl.BlockSpec((1, IDX_HEADS, tm, IDX_DIM), lambda bi, ti: (bi, 0, ti, 0)),
            pl.BlockSpec((1, tm, IDX_DIM), lambda bi, ti: (bi, ti, 0)),
            pl.BlockSpec((1, SUBLANES, tm), lambda bi, ti: (bi, 0, ti)),
        ],
        out_shape=[
            jax.ShapeDtypeStruct((b, ATT_HEADS, s, ATT_HEAD_DIM), BF16),
            jax.ShapeDtypeStruct((b, s, ATT_KV_HEADS * ATT_HEAD_DIM), BF16),
            jax.ShapeDtypeStruct((b, ATT_KV_HEADS, s // tm, ATT_HEAD_DIM, tm), BF16),
            jax.ShapeDtypeStruct((b, IDX_HEADS, s, IDX_DIM), BF16),
            jax.ShapeDtypeStruct((b, s, IDX_DIM), BF16),
            jax.ShapeDtypeStruct((b, SUBLANES, s), F32),
        ],
        compiler_params=pltpu.CompilerParams(
            dimension_semantics=("arbitrary", "arbitrary"), vmem_limit_bytes=VMEM_LIMIT),
        name="odd_proj",
    )(h, norm_g.reshape(1, d), w, pos, inv)


def _float_to_ordered(x):
    bits = lax.bitcast_convert_type(x, jnp.int32)
    return jnp.where(bits >= 0, bits, bits ^ jnp.int32(0x7FFFFFFF))


def _ordered_to_float(o):
    bits = jnp.where(o >= 0, o, o ^ jnp.int32(0x7FFFFFFF))
    return lax.bitcast_convert_type(bits, F32)


def _dsa_kernel(q_ref, qi_ref, wi_ref, k_ref, vt_ref, ki_ref, h_ref, wo_ref, upper_ref, o_ref,
                sc_ref, bias_ref, qe_ref, acc_ref, og_ref, sta_ref, stb_ref, pa_ref, pb_ref, *, tq, kc, top_k):
    qb = pl.program_id(1)
    n_kc = (qb * tq + tq + kc - 1) // kc
    rep = ATT_HEADS // ATT_KV_HEADS
    q_pos = qb * tq + lax.broadcasted_iota(jnp.int32, (1, tq), 1)
    scan_iota = lax.broadcasted_iota(jnp.int32, (SCAN_ROWS, tq), 0)

    qi = qi_ref[0].reshape(IDX_HEADS * tq, IDX_DIM)
    wi = wi_ref[0] * ((IDX_DIM ** -0.5) * (IDX_HEADS ** -0.5))

    def fold(vals, op):
        return op(vals.reshape(vals.shape[0] // COUNT_ROWS, COUNT_ROWS, tq), axis=0)

    def score_body(c, carry):
        mx_p, mn_p = carry
        for r0 in range(0, kc, SCAN_ROWS):
            off = pl.multiple_of(c * kc + r0, SCAN_ROWS)
            lg = jnp.maximum(_dot_nt(ki_ref[0, pl.ds(off, SCAN_ROWS), :], qi), 0.0)
            isc = lg[:, 0:tq] * wi[0:1, :]
            for hd in range(1, IDX_HEADS):
                isc = isc + lg[:, hd * tq:(hd + 1) * tq] * wi[hd:hd + 1, :]
            sc = jnp.where(scan_iota + off <= q_pos, isc, -jnp.inf)
            sc_ref[c, r0:r0 + SCAN_ROWS, :] = sc
            mx_p = jnp.maximum(mx_p, fold(sc, jnp.max))
            mn_p = jnp.minimum(mn_p, fold(isc, jnp.min))
        return mx_p, mn_p

    mx_p, mn_p = lax.fori_loop(0, n_kc, score_body, (jnp.full((COUNT_ROWS, tq), -jnp.inf, F32),
                                                     jnp.full((COUNT_ROWS, tq), jnp.inf, F32)))
    mx = jnp.max(mx_p, axis=0, keepdims=True) + 0.0
    mn = jnp.min(mn_p, axis=0, keepdims=True)

    def count_ge(t):
        def body(c, acc):
            for r0 in range(0, kc, SCAN_ROWS):
                xs = sc_ref[c, r0:r0 + SCAN_ROWS, :].reshape(SCAN_ROWS // COUNT_ROWS, COUNT_ROWS, tq)
                acc = acc + jnp.sum(jnp.where(xs >= t, 1.0, 0.0), axis=0)
            return acc
        part = lax.fori_loop(0, n_kc, body, jnp.zeros((COUNT_ROWS, tq), F32))
        return jnp.sum(part, axis=0, keepdims=True)

    def max_below(t):
        def body(c, acc):
            for r0 in range(0, kc, SCAN_ROWS):
                xs = sc_ref[c, r0:r0 + SCAN_ROWS, :]
                acc = jnp.maximum(acc, fold(jnp.where(xs < t, xs, -jnp.inf), jnp.max))
            return acc
        part = lax.fori_loop(0, n_kc, body, jnp.full((COUNT_ROWS, tq), -jnp.inf, F32))
        return jnp.max(part, axis=0, keepdims=True)

    kf = float(top_k)
    n_adm = (q_pos + 1).astype(F32)
    small = n_adm <= kf
    state = (jnp.where(small, 0.0, mn),
             jnp.where(small, 1.0, _ordered_to_float(_float_to_ordered(mx) + 1)),
             n_adm,
             jnp.where(small, 1.0, 0.0),
             jnp.full((1, tq), -jnp.inf, F32),
             jnp.full((1, tq), kf, F32))

    def bis_body(_, st):
        lo, hi, c_lo, done, thr, c_thr = st
        mid = 0.5 * lo + 0.5 * hi
        c = count_ge(mid)
        active = done == 0.0
        stuck = (mid <= lo) | (mid >= hi)
        fin_stuck = active & stuck
        fin_hit = active & (~stuck) & (c == kf)
        up = active & (~stuck) & (c >= kf)
        down = active & (~stuck) & (c < kf)
        thr = jnp.where(fin_stuck, lo, jnp.where(fin_hit, mid, thr))
        c_thr = jnp.where(fin_stuck, c_lo, c_thr)
        done = jnp.where(fin_stuck | fin_hit, 1.0, done)
        return (jnp.where(up, mid, lo), jnp.where(down, mid, hi), jnp.where(up, c, c_lo), done, thr, c_thr)

    _, hi, _, done, thr, c_thr = lax.fori_loop(0, BISECT_STEPS, bis_body, state)

    def peel_body(st):
        _, hi, done, thr, c_thr = st
        v = max_below(hi)
        c = count_ge(v)
        active = done == 0.0
        ok = active & (c >= kf)
        done = jnp.where(ok, 1.0, done)
        return (jnp.sum(1.0 - done), jnp.where(active & (c < kf), v, hi), done,
                jnp.where(ok, v, thr), jnp.where(ok, c, c_thr))

    _, _, _, thr, c_thr = lax.while_loop(lambda st: st[0] > 0.0, peel_body,
                                         (jnp.sum(1.0 - done), hi, done, thr, c_thr))

    ka = kc // 2
    excess = c_thr - kf
    half_iota = lax.broadcasted_iota(jnp.int32, (ka, tq), 0)

    def mask_body(j, later_total):
        c = n_kc - 1 - j
        halves = []
        for half in (1, 0):
            xs = sc_ref[c, half * ka:(half + 1) * ka, :]
            ties = jnp.where(xs == thr, 1.0, 0.0)
            halves.append((half, xs, ties, _dot(upper_ref[...], ties.astype(BF16))))
        for half, xs, ties, later_in_half in halves:
            later = later_in_half + later_total
            sel = jnp.where(xs > thr, 0.0,
                            jnp.where(ties > 0.0, jnp.where(later >= excess, 0.0, MASKED_SCORE), MASKED_SCORE))
            kidx = half_iota + (c * kc + half * ka)
            bias_ref[c, half * ka:(half + 1) * ka, :] = jnp.where(kidx <= q_pos, sel, MASKED_SCORE).astype(BF16)
            later_total = later[0:1, :] + ties[0:1, :]
        return later_total

    lax.fori_loop(0, n_kc, mask_body, jnp.zeros((1, tq), F32))

    acc_ref[...] = jnp.zeros_like(acc_ref)
    ones_rows = jnp.ones((BF16_SUBLANES, ka), BF16)
    n_qs = tq // LANES
    own_col = ((lax.broadcasted_iota(jnp.int32, (rep * LANES, LANES), 0) & (LANES - 1))
               == lax.broadcasted_iota(jnp.int32, (rep * LANES, LANES), 1))
    for gi in range(ATT_KV_HEADS):
        for sb in range(n_qs):
            qs = q_ref[0, gi * rep:(gi + 1) * rep, sb * LANES:(sb + 1) * LANES, :]
            qe_ref[gi, sb, :, 0:ATT_HEAD_DIM] = qs.reshape(rep * LANES, ATT_HEAD_DIM)
            qe_ref[gi, sb, :, ATT_HEAD_DIM:] = jnp.where(own_col, 1.0, 0.0).astype(BF16)

    def scores_into(dst_ref, c, half):
        off = pl.multiple_of(c * kc + half * ka, ka)
        bias = bias_ref[c, half * ka:(half + 1) * ka, :]
        tops = []
        for gi in range(ATT_KV_HEADS):
            kb = k_ref[0, pl.ds(off, ka), gi * ATT_HEAD_DIM:(gi + 1) * ATT_HEAD_DIM]
            st = jnp.concatenate(
                [_dot_nt(jnp.concatenate([kb, bias[:, sb * LANES:(sb + 1) * LANES]], axis=1), qe_ref[gi, sb])
                 for sb in range(n_qs)], axis=1)
            dst_ref[gi] = st
            tops.append(jnp.max(st, axis=0, keepdims=True))
        return tuple(tops)

    def softmax_into(p_ref, src_ref, m_all, tops):
        m_out, alphas = [], []
        for gi in range(ATT_KV_HEADS):
            m_new = jnp.maximum(m_all[gi], tops[gi])
            alphas.append(jnp.exp2(m_all[gi] - m_new))
            p_ref[gi] = jnp.exp2(src_ref[gi] - m_new).astype(BF16)
            m_out.append(m_new)
        return tuple(m_out), tuple(alphas)

    def values_in(p_ref, c, half, alphas):
        for gi in range(ATT_KV_HEADS):
            vt = jnp.concatenate([vt_ref[0, gi, c, :, half * ka:(half + 1) * ka], ones_rows], axis=0)
            acc_ref[gi] = alphas[gi] * acc_ref[gi] + _dot(vt, p_ref[gi])

    def att_body(c, carry):
        m_all, tops_a, alpha_b = carry
        tops_b = scores_into(stb_ref, c, 1)
        values_in(pb_ref, jnp.maximum(c - 1, 0), 1, alpha_b)
        m_all, alpha_a = softmax_into(pa_ref, sta_ref, m_all, tops_a)
        tops_a = scores_into(sta_ref, jnp.minimum(c + 1, n_kc - 1), 0)
        values_in(pa_ref, c, 0, alpha_a)
        m_all, alpha_b = softmax_into(pb_ref, stb_ref, m_all, tops_b)
        return m_all, tops_a, alpha_b

    pb_ref[...] = jnp.zeros_like(pb_ref)
    m0 = (jnp.full((1, rep * tq), MASKED_SCORE, F32),) * ATT_KV_HEADS
    one = (jnp.ones((1, rep * tq), F32),) * ATT_KV_HEADS
    _, _, alpha_b = lax.fori_loop(0, n_kc, att_body, (m0, scores_into(sta_ref, 0, 0), one))
    values_in(pb_ref, n_kc - 1, 1, alpha_b)
    for gi in range(ATT_KV_HEADS):
        acc = acc_ref[gi]
        og = (acc[0:ATT_HEAD_DIM] / acc[ATT_HEAD_DIM:ATT_HEAD_DIM + 1]).astype(BF16)
        for sb in range(n_qs):
            for r in range(rep):
                hd = gi * rep + r
                col = (sb * rep + r) * LANES
                og_ref[hd * ATT_HEAD_DIM:(hd + 1) * ATT_HEAD_DIM, sb * LANES:(sb + 1) * LANES] = og[:, col:col + LANES]

    o_ref[0] = h_ref[0] + _dot_tn(og_ref[...], wo_ref[...])


def _dsa_key_chunk(s):
    return min(DSA_KEY_CHUNK, s)


def _dsa(h, q, k, v_t, qi, ki, wi_t, w_out):
    b, s, d = h.shape
    tq = DSA_QUERY_TILE
    top_k = min(TOPK_MAX, s // 4)
    kc = _dsa_key_chunk(s)
    assert kc >= top_k and s % kc == 0 and kc % (2 * tq) == 0
    rep = ATT_HEADS // ATT_KV_HEADS
    nkv = ATT_KV_HEADS * ATT_HEAD_DIM
    n_chunks = s // kc
    vrows = ATT_HEAD_DIM + BF16_SUBLANES
    ka = kc // 2
    upper = jnp.asarray(np.arange(ka)[None, :] > np.arange(ka)[:, None], BF16)
    return pl.pallas_call(
        functools.partial(_dsa_kernel, tq=tq, kc=kc, top_k=top_k),
        grid=(b, s // tq),
        in_specs=[
            pl.BlockSpec((1, ATT_HEADS, tq, ATT_HEAD_DIM), lambda bi, qi_: (bi, 0, qi_, 0)),
            pl.BlockSpec((1, IDX_HEADS, tq, IDX_DIM), lambda bi, qi_: (bi, 0, qi_, 0)),
            pl.BlockSpec((1, SUBLANES, tq), lambda bi, qi_: (bi, 0, qi_)),
            pl.BlockSpec((1, s, nkv), lambda bi, qi_: (bi, 0, 0)),
            pl.BlockSpec((1, ATT_KV_HEADS, n_chunks, ATT_HEAD_DIM, kc), lambda bi, qi_: (bi, 0, 0, 0, 0)),
            pl.BlockSpec((1, s, IDX_DIM), lambda bi, qi_: (bi, 0, 0)),
            pl.BlockSpec((1, tq, d), lambda bi, qi_: (bi, qi_, 0)),
            _const_spec((ATT_HEADS * ATT_HEAD_DIM, d)),
            _const_spec((ka, ka)),
        ],
        out_specs=pl.BlockSpec((1, tq, d), lambda bi, qi_: (bi, qi_, 0)),
        out_shape=jax.ShapeDtypeStruct((b, s, d), F32),
        scratch_shapes=[
            pltpu.VMEM((n_chunks, kc, tq), F32),
            pltpu.VMEM((n_chunks, kc, tq), BF16),
            pltpu.VMEM((ATT_KV_HEADS, tq // LANES, rep * LANES, 2 * ATT_HEAD_DIM), BF16),
            pltpu.VMEM((ATT_KV_HEADS, vrows, rep * tq), F32),
            pltpu.VMEM((ATT_HEADS * ATT_HEAD_DIM, tq), BF16),
            pltpu.VMEM((ATT_KV_HEADS, kc // 2, rep * tq), F32),
            pltpu.VMEM((ATT_KV_HEADS, kc // 2, rep * tq), F32),
            pltpu.VMEM((ATT_KV_HEADS, kc // 2, rep * tq), BF16),
            pltpu.VMEM((ATT_KV_HEADS, kc // 2, rep * tq), BF16),
        ],
        compiler_params=pltpu.CompilerParams(
            dimension_semantics=("arbitrary", "arbitrary"), vmem_limit_bytes=VMEM_LIMIT),
        name="dsa",
    )(q, qi, wi_t, k, v_t, ki, h, w_out.astype(BF16), upper)


def kernel(x, positions, ffn1_norm, ffn1_wi, ffn1_wo, mix_norm, ffn2_norm, ffn2_wi, ffn2_wo, even_w_in, gla_gate_w, gla_gate_b, gla_out_norm, pool_w, pool_scale, even_w_out, odd_w_in, odd_w_out, final_norm):
    b, s, d = x.shape
    depth = ffn1_wi.shape[0]
    h = x
    for li in range(depth):
        h = _ffn(h.reshape(b * s, d), ffn1_norm[li], ffn1_wi, ffn1_wo, li, final_norm, final=False)
        h = h.reshape(b, s, d)
        j = li // 2
        if li % 2 == 0:
            h = _even_mixer(h, mix_norm[li], even_w_in[j], gla_gate_w[j], gla_gate_b[j], gla_out_norm[j],
                            pool_w[j], pool_scale[j], even_w_out[j])
        else:
            q, k, v_t, qi, ki, wi_t = _odd_proj(h, mix_norm[li], odd_w_in[j], positions)
            h = _dsa(h, q, k, v_t, qi, ki, wi_t, odd_w_out[j])
        last = li == depth - 1
        h = _ffn(h.reshape(b * s, d), ffn2_norm[li], ffn2_wi, ffn2_wo, li, final_norm, final=last)
        h = h.reshape(b, s, d)
    return h
```
